```python
import math, functools
import jax, jax.numpy as jnp
from jax import lax
import numpy as np


D_MODEL = 4096
BATCH = 4
SEQ = 2048
DEPTH = 2
DEC_BATCH = 8
DEC_SEQ = 4
PAST_LEN = 16384
PAGE_SIZE = 128

HEAD_DIM = 128
A_HEADS = 16
A_WIDTH = A_HEADS * HEAD_DIM
MOBA_BLOCK = 256
MOBA_TOPK = 3
QUERY_BLOCK = 16
NUM_BUCKETS = 32
MAX_DISTANCE = 128
B_HEADS = 16
HG_K = 128
HG_V = 128
B_WIDTH = B_HEADS * HG_V
HG_CHUNK = 64
D_FF = 4 * D_MODEL
IN_COLS = 3 * A_WIDTH + 2 * B_HEADS * HG_K + 2 * B_WIDTH + 2 * D_MODEL
EPS = 1e-6

kernel_name = 'moba_hgrn2_gated_hybrid_step'


def rmsnorm(x, g):
    x32 = x.astype(jnp.float32)
    y = x32 * lax.rsqrt(jnp.mean(x32 * x32, axis=-1, keepdims=True) + EPS)
    return (y * g.astype(jnp.float32)).astype(x.dtype)


def t5_bucket(rel):
    n = jnp.maximum(-rel, 0)
    max_exact = NUM_BUCKETS // 2
    nf = jnp.maximum(n, 1).astype(jnp.float32)
    far = max_exact + (jnp.log(nf / max_exact) / math.log(MAX_DISTANCE / max_exact)
                       * (NUM_BUCKETS - max_exact)).astype(jnp.int32)
    far = jnp.minimum(far, NUM_BUCKETS - 1)
    return jnp.where(n < max_exact, n, far)


def moba_core(q, q_pos, k_own, v_own, own_pos, k_sel, v_sel, sel_pos, sel_valid, rel_bias):
    H = q.shape[1]
    scale = HEAD_DIM ** -0.5
    bias_t = rel_bias.astype(jnp.float32).T
    own_rel = own_pos[None, :] - q_pos[:, None]
    own_logit = (jnp.einsum('bhqd,bhkd->bhqk', q, k_own).astype(jnp.float32) * scale
                 + bias_t[:, t5_bucket(own_rel)][None])
    own_logit = jnp.where(own_rel <= 0, own_logit, -jnp.inf)
    if k_sel is None:
        p = jax.nn.softmax(own_logit, axis=-1).astype(v_own.dtype)
        return jnp.einsum('bhqk,bhkd->bhqd', p, v_own)
    h_idx = jnp.arange(H)[None, :, None, None]
    sel_rel = sel_pos - q_pos[:, None]
    sel_logit = (jnp.einsum('bhqd,bhqnd->bhqn', q, k_sel).astype(jnp.float32) * scale
                 + bias_t[h_idx, t5_bucket(sel_rel)])
    sel_logit = jnp.where(sel_valid, sel_logit, -jnp.inf)
    ns = sel_logit.shape[-1]
    p = jax.nn.softmax(jnp.concatenate([sel_logit, own_logit], axis=-1), axis=-1).astype(v_own.dtype)
    return (jnp.einsum('bhqn,bhqnd->bhqd', p[..., :ns], v_sel)
            + jnp.einsum('bhqk,bhkd->bhqd', p[..., ns:], v_own))


def moba_prompt(q, k, v, rel_bias):
    B, T, H, D = q.shape
    n_blk = -(-T // MOBA_BLOCK)
    pad = n_blk * MOBA_BLOCK - T
    qh = q.transpose(0, 2, 1, 3)
    k_blocks = jnp.pad(k.transpose(0, 2, 1, 3), ((0, 0), (0, 0), (0, pad), (0, 0))).reshape(B, H, n_blk, MOBA_BLOCK, D)
    v_blocks = jnp.pad(v.transpose(0, 2, 1, 3), ((0, 0), (0, 0), (0, pad), (0, 0))).reshape(B, H, n_blk, MOBA_BLOCK, D)
    n_top = min(MOBA_TOPK, n_blk - 1)
    if n_top > 0:
        q_blk = jnp.arange(T) // MOBA_BLOCK
        k_mean = jnp.mean(k_blocks, axis=3, dtype=jnp.float32)
        gate = jnp.einsum('bhtd,bhnd->bhtn', qh.astype(jnp.float32), k_mean)
        gate = jnp.where(jnp.arange(n_blk)[None, :] < q_blk[:, None], gate, -jnp.inf)
        _, top_idx = lax.top_k(gate, n_top)
        top_valid = top_idx < q_blk[:, None]
    b_idx = jnp.arange(B)[:, None, None, None]
    h_idx = jnp.arange(H)[None, :, None, None]

    def one_query_block(i):
        start = i * QUERY_BLOCK
        q_i = lax.dynamic_slice_in_dim(qh, start, QUERY_BLOCK, axis=2)
        q_pos = start + jnp.arange(QUERY_BLOCK)
        j = start // MOBA_BLOCK
        k_own = lax.dynamic_index_in_dim(k_blocks, j, axis=2, keepdims=False)
        v_own = lax.dynamic_index_in_dim(v_blocks, j, axis=2, keepdims=False)
        own_pos = j * MOBA_BLOCK + jnp.arange(MOBA_BLOCK)
        if n_top == 0:
            return moba_core(q_i, q_pos, k_own, v_own, own_pos, None, None, None, None, rel_bias)
        idx = lax.dynamic_slice_in_dim(top_idx, start, QUERY_BLOCK, axis=2)
        valid = lax.dynamic_slice_in_dim(top_valid, start, QUERY_BLOCK, axis=2)
        ns = n_top * MOBA_BLOCK
        k_sel = k_blocks[b_idx, h_idx, idx].reshape(B, H, QUERY_BLOCK, ns, D)
        v_sel = v_blocks[b_idx, h_idx, idx].reshape(B, H, QUERY_BLOCK, ns, D)
        sel_pos = (idx[..., None] * MOBA_BLOCK + jnp.arange(MOBA_BLOCK)).reshape(B, H, QUERY_BLOCK, ns)
        sel_valid = jnp.broadcast_to(valid[..., None], idx.shape + (MOBA_BLOCK,)).reshape(B, H, QUERY_BLOCK, ns)
        return moba_core(q_i, q_pos, k_own, v_own, own_pos, k_sel, v_sel, sel_pos, sel_valid, rel_bias)

    out = lax.map(one_query_block, jnp.arange(T // QUERY_BLOCK))
    return out.transpose(1, 0, 3, 2, 4).reshape(B, T, H * D)


def moba_sample(q, k, v, cache_k, cache_v, page_table, rel_bias, layer):
    DB, Tn, H, D = q.shape
    n_pages = page_table.shape[1]
    past = n_pages * PAGE_SIZE
    ppb = MOBA_BLOCK // PAGE_SIZE
    n_full = past // MOBA_BLOCK
    bstart = n_full * MOBA_BLOCK
    q_pos = past + jnp.arange(Tn)
    qh = q.transpose(0, 2, 1, 3)
    kn = k.transpose(0, 2, 1, 3)
    vn = v.transpose(0, 2, 1, 3)
    if past > bstart:
        own_pages = page_table[:, bstart // PAGE_SIZE:]
        k_c = cache_k[layer, own_pages].reshape(DB, past - bstart, H, D).transpose(0, 2, 1, 3)
        v_c = cache_v[layer, own_pages].reshape(DB, past - bstart, H, D).transpose(0, 2, 1, 3)
        k_own = jnp.concatenate([k_c.astype(kn.dtype), kn], axis=2)
        v_own = jnp.concatenate([v_c.astype(vn.dtype), vn], axis=2)
    else:
        k_own, v_own = kn, vn
    own_pos = bstart + jnp.arange(past - bstart + Tn)
    n_top = min(MOBA_TOPK, n_full)
    if n_top == 0:
        out = moba_core(qh, q_pos, k_own, v_own, own_pos, None, None, None, None, rel_bias)
        return out.transpose(0, 2, 1, 3).reshape(DB, Tn, H * D)
    k_past = cache_k[layer, page_table[:, :n_full * ppb]]
    k_mean = jnp.mean(k_past.reshape(DB, n_full, MOBA_BLOCK, H, D), axis=2, dtype=jnp.float32)
    gate = jnp.einsum('bhtd,bnhd->bhtn', qh.astype(jnp.float32), k_mean)
    _, top_idx = lax.top_k(gate, n_top)
    logical = top_idx[..., None] * ppb + jnp.arange(ppb)
    phys = page_table[jnp.arange(DB)[:, None, None, None, None], logical]
    h_idx = jnp.arange(H)[None, :, None, None, None]
    ns = n_top * MOBA_BLOCK
    k_sel = cache_k[layer, phys, :, h_idx].reshape(DB, H, Tn, ns, D).astype(kn.dtype)
    v_sel = cache_v[layer, phys, :, h_idx].reshape(DB, H, Tn, ns, D).astype(vn.dtype)
    sel_pos = (top_idx[..., None] * MOBA_BLOCK + jnp.arange(MOBA_BLOCK)).reshape(DB, H, Tn, ns)
    sel_valid = jnp.ones(sel_pos.shape, dtype=bool)
    out = moba_core(qh, q_pos, k_own, v_own, own_pos, k_sel, v_sel, sel_pos, sel_valid, rel_bias)
    return out.transpose(0, 2, 1, 3).reshape(DB, Tn, H * D)


def gla_chunked(q, k, v, logf, s0):
    B, T, H, K = q.shape
    V = v.shape[-1]
    C = math.gcd(HG_CHUNK, T)
    n = T // C

    def to_chunks(a):
        return a.reshape(B, n, C, H, a.shape[-1]).transpose(1, 0, 2, 3, 4)

    causal = jnp.tril(jnp.ones((C, C), dtype=bool))[None, :, :, None, None]

    def step(S, xs):
        qc, kc, vc, gc = xs
        b = jnp.cumsum(gc, axis=1)
        decay = jnp.exp(jnp.where(causal, b[:, :, None] - b[:, None, :], -jnp.inf))
        scores = jnp.einsum('bthk,bshk,btshk->bhts', qc, kc, decay)
        o = (jnp.einsum('bhts,bshv->bthv', scores, vc)
             + jnp.einsum('bthk,bhkv->bthv', qc * jnp.exp(b), S))
        b_last = b[:, -1]
        S = (jnp.exp(b_last)[..., None] * S
             + jnp.einsum('bshk,bshv->bhkv', kc * jnp.exp(b_last[:, None] - b), vc))
        return S, o

    s_fin, o = lax.scan(step, s0, (to_chunks(q), to_chunks(k), to_chunks(v), to_chunks(logf)))
    return o.transpose(1, 0, 2, 3, 4).reshape(B, T, H, V), s_fin


def hgrn2_branch(qb, fb, ib, gb, s0, lb, gain):
    B, T, _ = qb.shape
    f32 = jnp.float32
    q = jax.nn.silu(qb.astype(f32)).reshape(B, T, B_HEADS, HG_K)
    z = fb.astype(f32)
    if lb is None:
        logf = jax.nn.log_sigmoid(z)
    else:
        logf = jnp.log(lb + (1.0 - lb) * jax.nn.sigmoid(z))
    logf = logf.reshape(B, T, B_HEADS, HG_K)
    k = -jnp.expm1(logf)
    v = ib.astype(f32).reshape(B, T, B_HEADS, HG_V)
    o, s_new = gla_chunked(q, k, v, logf, s0.astype(f32))
    o = rmsnorm(o, gain) * jax.nn.silu(gb.astype(f32).reshape(B, T, B_HEADS, HG_V))
    return o.reshape(B, T, B_WIDTH).astype(qb.dtype), s_new


def trunk_layer(x, s0, moba_fn, lb, w_in_l, w_ba_l, w_bb_l, w_out_l, w_up_l, w_down_l, nmix_l, nmlp_l, hgn_l):
    B, T, _ = x.shape
    h = rmsnorm(x, nmix_l)
    proj = h @ w_in_l
    widths = (A_WIDTH, A_WIDTH, A_WIDTH, B_HEADS * HG_K, B_HEADS * HG_K, B_WIDTH, B_WIDTH, D_MODEL, D_MODEL)
    points = [int(p) for p in np.cumsum(widths)[:-1]]
    qa, ka, va, qb, fb, ib, gb, gate_a, gate_b = jnp.split(proj, points, axis=-1)
    qa = qa.reshape(B, T, A_HEADS, HEAD_DIM)
    ka = ka.reshape(B, T, A_HEADS, HEAD_DIM)
    va = va.reshape(B, T, A_HEADS, HEAD_DIM)
    o_a = moba_fn(qa, ka, va)
    o_b, s_new = hgrn2_branch(qb, fb, ib, gb, s0, lb, hgn_l)
    merged = jax.nn.sigmoid(gate_a) * (o_a @ w_ba_l) + jax.nn.sigmoid(gate_b) * (o_b @ w_bb_l)
    x = x + merged @ w_out_l
    h2 = rmsnorm(x, nmlp_l)
    x = x + jnp.square(jax.nn.relu(h2 @ w_up_l)) @ w_down_l
    return x, ka, va, s_new


def setup_inputs(seed: int = 0) -> dict:
    key = jax.random.key(seed)
    ks = jax.random.split(key, 20)
    n_pages = PAST_LEN // PAGE_SIZE
    n_pool = (DEC_BATCH * n_pages * 5) // 4
    nrm = jax.random.normal
    x_prompt = nrm(ks[0], (BATCH, SEQ, D_MODEL), jnp.float32)
    x_sample = nrm(ks[1], (DEC_BATCH, DEC_SEQ, D_MODEL), jnp.float32)
    cache_k = nrm(ks[2], (DEPTH, n_pool, PAGE_SIZE, A_HEADS, HEAD_DIM), jnp.float32)
    cache_v = nrm(ks[3], (DEPTH, n_pool, PAGE_SIZE, A_HEADS, HEAD_DIM), jnp.float32)
    state_hgrn = 0.5 * nrm(ks[4], (DEPTH, DEC_BATCH, B_HEADS, HG_K, HG_V), jnp.float32)
    page_table = jax.random.permutation(ks[5], n_pool)[:DEC_BATCH * n_pages].reshape(DEC_BATCH, n_pages).astype(jnp.int32)
    w_in = nrm(ks[6], (DEPTH, D_MODEL, IN_COLS), jnp.float32) * D_MODEL ** -0.5
    w_branch_a = nrm(ks[7], (DEPTH, A_WIDTH, D_MODEL), jnp.float32) * A_WIDTH ** -0.5
    w_branch_b = nrm(ks[8], (DEPTH, B_WIDTH, D_MODEL), jnp.float32) * B_WIDTH ** -0.5
    w_out = nrm(ks[9], (DEPTH, D_MODEL, D_MODEL), jnp.float32) * D_MODEL ** -0.5
    w_up = nrm(ks[10], (DEPTH, D_MODEL, D_FF), jnp.float32) * D_MODEL ** -0.5
    w_down = nrm(ks[11], (DEPTH, D_FF, D_MODEL), jnp.float32) * D_FF ** -0.5
    norm_mix = 1.0 + 0.02 * nrm(ks[12], (DEPTH, D_MODEL), jnp.float32)
    norm_mlp = 1.0 + 0.02 * nrm(ks[13], (DEPTH, D_MODEL), jnp.float32)
    hg_norm = 1.0 + 0.02 * nrm(ks[14], (DEPTH, HG_V), jnp.float32)
    hg_lower_bounds = nrm(ks[15], (DEPTH, B_HEADS * HG_K), jnp.float32)
    rel_bias = 0.1 * nrm(ks[16], (NUM_BUCKETS, A_HEADS), jnp.float32)
    final_norm = 1.0 + 0.02 * nrm(ks[17], (D_MODEL,), jnp.float32)
    return {'x_prompt': x_prompt, 'x_sample': x_sample, 'cache_k': cache_k, 'cache_v': cache_v,
            'state_hgrn': state_hgrn, 'page_table': page_table, 'w_in': w_in,
            'w_branch_a': w_branch_a, 'w_branch_b': w_branch_b, 'w_out': w_out, 'w_up': w_up,
            'w_down': w_down, 'norm_mix': norm_mix, 'norm_mlp': norm_mlp, 'hg_norm': hg_norm,
            'hg_lower_bounds': hg_lower_bounds, 'rel_bias': rel_bias, 'final_norm': final_norm}


def reference(x_prompt, x_sample, cache_k, cache_v, state_hgrn, page_table, w_in, w_branch_a,
              w_branch_b, w_out, w_up, w_down, norm_mix, norm_mlp, hg_norm, hg_lower_bounds,
              rel_bias, final_norm):
    p_lb = jax.nn.softmax(hg_lower_bounds.astype(jnp.float32), axis=0)
    lower = jnp.cumsum(p_lb, axis=0) - p_lb[0]
    xp, xs = x_prompt, x_sample
    s_p0 = jnp.zeros((x_prompt.shape[0], B_HEADS, HG_K, HG_V), jnp.float32)
    kp_l, vp_l, sp_l, ks_l, vs_l, ss_l = [], [], [], [], [], []
    for l in range(DEPTH):
        lb = None if l == 0 else lower[l]
        moba_p = functools.partial(moba_prompt, rel_bias=rel_bias)
        moba_s = functools.partial(moba_sample, cache_k=cache_k, cache_v=cache_v,
                                   page_table=page_table, rel_bias=rel_bias, layer=l)
        xp, kp, vp, sp = trunk_layer(xp, s_p0, moba_p, lb, w_in[l], w_branch_a[l], w_branch_b[l],
                                     w_out[l], w_up[l], w_down[l], norm_mix[l], norm_mlp[l], hg_norm[l])
        xs, kss, vss, sss = trunk_layer(xs, state_hgrn[l], moba_s, lb, w_in[l], w_branch_a[l], w_branch_b[l],
                                        w_out[l], w_up[l], w_down[l], norm_mix[l], norm_mlp[l], hg_norm[l])
        kp_l.append(kp)
        vp_l.append(vp)
        sp_l.append(sp)
        ks_l.append(kss)
        vs_l.append(vss)
        ss_l.append(sss)
    y_prompt = rmsnorm(xp, final_norm)
    y_sample = rmsnorm(xs, final_norm)
    new_k_prompt = jnp.stack(kp_l)
    new_v_prompt = jnp.stack(vp_l)
    new_state_prompt = jnp.stack(sp_l)
    new_k_sample = jnp.stack(ks_l)
    new_v_sample = jnp.stack(vs_l)
    new_state_sample = jnp.stack(ss_l)
    return (y_prompt, y_sample, new_k_prompt, new_v_prompt, new_state_prompt, new_k_sample, new_v_sample, new_state_sample)
```

```python
import functools
import math

import jax
import jax.numpy as jnp
from jax import lax
from jax.experimental import pallas as pl
from jax.experimental.pallas import tpu as pltpu

HEAD_DIM = 128
MOBA_BLOCK = 256
MOBA_TOPK = 3
NUM_BUCKETS = 32
MAX_DISTANCE = 128
PAGE_SIZE = 128
HG_K = 128
HG_V = 128
EPS = 1e-6

V7X_VMEM_LIMIT_BYTES = 60000 * 1024
SUBLANES = 8
LANES = 128

GLA_CHUNK = 256
NEG_INF = float("-inf")


def _compiler_params(semantics, vmem_bytes):
    limit = int(min(max(vmem_bytes, 16 * 1024 * 1024), V7X_VMEM_LIMIT_BYTES))
    return pltpu.CompilerParams(dimension_semantics=semantics, vmem_limit_bytes=limit)


def _rmsnorm_body(x_ref, g_ref, o_ref):
    x = x_ref[...]
    ms = jnp.mean(x * x, axis=-1, keepdims=True)
    o_ref[...] = (x * lax.rsqrt(ms + EPS) * g_ref[...]).astype(o_ref.dtype)


def _rmsnorm(x, gain, out_dtype):
    m, d = x.shape
    tm = min(m, 256)
    assert m % tm == 0
    block_bytes = tm * d * (4 + jnp.dtype(out_dtype).itemsize)
    return pl.pallas_call(
        _rmsnorm_body,
        out_shape=jax.ShapeDtypeStruct((m, d), out_dtype),
        grid=(m // tm,),
        in_specs=[pl.BlockSpec((tm, d), lambda i: (i, 0)),
                  pl.BlockSpec((1, d), lambda i: (0, 0))],
        out_specs=pl.BlockSpec((tm, d), lambda i: (i, 0)),
        compiler_params=_compiler_params(("parallel",), 4 * block_bytes),
        name="rmsnorm",
    )(x, gain.reshape(1, d))


def _act_none(a):
    return a


def _act_sigmoid(a):
    return jax.nn.sigmoid(a)


def _act_relu2(a):
    return jnp.square(jnp.maximum(a, 0.0))


def _mm_body(x_ref, w_ref, *refs, act, has_res, nk):
    o_ref = refs[-1]
    acc = jnp.dot(x_ref[...], w_ref[...], preferred_element_type=jnp.float32)
    if nk == 1:
        if has_res:
            acc = acc + refs[0][...]
        o_ref[...] = act(acc).astype(o_ref.dtype)
    else:
        k = pl.program_id(2)

        @pl.when(k == 0)
        def _():
            o_ref[...] = (acc + refs[0][...]) if has_res else acc

        @pl.when(k > 0)
        def _():
            o_ref[...] += acc


def _matmul(x, w, *, col0=0, ncols=None, out_dtype, act=_act_none, residual=None, name):
    m, kdim = x.shape
    ncols = w.shape[1] if ncols is None else ncols
    tm = min(m, 1024)
    tk = min(kdim, 4096)
    tn = math.gcd(ncols, col0, 2048 if m <= 256 else 1024)
    assert m % tm == 0 and kdim % tk == 0 and ncols % tn == 0 and col0 % tn == 0
    nm, nn, nk = m // tm, ncols // tn, kdim // tk
    assert nk == 1 or (out_dtype == jnp.float32 and act is _act_none)
    c0 = col0 // tn
    in_specs = [pl.BlockSpec((tm, tk), lambda n, i, k: (i, k)),
                pl.BlockSpec((tk, tn), lambda n, i, k: (k, n + c0))]
    args = [x, w]
    osize = jnp.dtype(out_dtype).itemsize
    vmem = 2 * (tm * tk * 2 + tk * tn * 2 + tm * tn * osize) + 2 * tm * tn * 4
    if residual is not None:
        in_specs.append(pl.BlockSpec((tm, tn), lambda n, i, k: (i, n)))
        args.append(residual)
        vmem += 2 * tm * tn * 4
    return pl.pallas_call(
        functools.partial(_mm_body, act=act, has_res=residual is not None, nk=nk),
        out_shape=jax.ShapeDtypeStruct((m, ncols), out_dtype),
        grid=(nn, nm, nk),
        in_specs=in_specs,
        out_specs=pl.BlockSpec((tm, tn), lambda n, i, k: (i, n)),
        compiler_params=_compiler_params(("parallel", "parallel", "arbitrary"),
                                         vmem + 4 * 1024 * 1024),
        name=name,
    )(*args)


def _merge_body(oa_ref, ob_ref, wa_ref, wb_ref, ga_ref, gb_ref, o_ref):
    a = jnp.dot(oa_ref[...], wa_ref[...], preferred_element_type=jnp.float32)
    b = jnp.dot(ob_ref[...], wb_ref[...], preferred_element_type=jnp.float32)
    o_ref[...] = (ga_ref[...].astype(jnp.float32) * a
                  + gb_ref[...].astype(jnp.float32) * b).astype(o_ref.dtype)


def _merge_branches(o_a, o_b, w_ba, w_bb, gates):
    m, ka = o_a.shape
    kb = o_b.shape[1]
    d = w_ba.shape[1]
    tm = min(m, 1024)
    tn = min(d, 1024 if m <= 256 else 512)
    assert m % tm == 0 and d % tn == 0
    goff = d // tn
    vmem = 2 * 2 * (tm * ka + tm * kb + ka * tn + kb * tn + 3 * tm * tn) + 3 * tm * tn * 4
    return pl.pallas_call(
        _merge_body,
        out_shape=jax.ShapeDtypeStruct((m, d), jnp.bfloat16),
        grid=(d // tn, m // tm),
        in_specs=[pl.BlockSpec((tm, ka), lambda n, i: (i, 0)),
                  pl.BlockSpec((tm, kb), lambda n, i: (i, 0)),
                  pl.BlockSpec((ka, tn), lambda n, i: (0, n)),
                  pl.BlockSpec((kb, tn), lambda n, i: (0, n)),
                  pl.BlockSpec((tm, tn), lambda n, i: (i, n)),
                  pl.BlockSpec((tm, tn), lambda n, i: (i, n + goff))],
        out_specs=pl.BlockSpec((tm, tn), lambda n, i: (i, n)),
        compiler_params=_compiler_params(("parallel", "parallel"), vmem + 4 * 1024 * 1024),
        name="merge_branches",
    )(o_a, o_b, w_ba, w_bb, gates, gates)


def _t5_bias(dist, bias_ref, head):
    max_exact = NUM_BUCKETS // 2
    nf = jnp.maximum(dist, 1).astype(jnp.float32)
    far = max_exact + (jnp.log(nf / max_exact) / math.log(MAX_DISTANCE / max_exact)
                       * (NUM_BUCKETS - max_exact)).astype(jnp.int32)
    far = jnp.minimum(far, NUM_BUCKETS - 1)
    bucket = jnp.where(dist < max_exact, dist, far)
    out = jnp.zeros(dist.shape, jnp.float32)
    for k in range(NUM_BUCKETS):
        out = jnp.where(bucket == k, bias_ref[head, k], out)
    return out


def _bias_tiles_body(bias_ref, o_ref):
    h = pl.program_id(0)
    row = lax.broadcasted_iota(jnp.int32, (MOBA_BLOCK, MOBA_BLOCK), 0)
    col = lax.broadcasted_iota(jnp.int32, (MOBA_BLOCK, MOBA_BLOCK), 1)
    own = _t5_bias(jnp.maximum(row - col, 0), bias_ref, h)
    o_ref[0, 0] = jnp.where(row >= col, own, NEG_INF)
    o_ref[0, 1] = _t5_bias(row - col + MOBA_BLOCK, bias_ref, h)


def _bias_tiles(bias_t):
    h = bias_t.shape[0]
    return pl.pallas_call(
        _bias_tiles_body,
        out_shape=jax.ShapeDtypeStruct((h, 2, MOBA_BLOCK, MOBA_BLOCK), jnp.float32),
        grid_spec=pltpu.PrefetchScalarGridSpec(
            num_scalar_prefetch=1,
            grid=(h,),
            in_specs=[],
            out_specs=pl.BlockSpec((1, 2, MOBA_BLOCK, MOBA_BLOCK), lambda i, b: (i, 0, 0, 0)),
        ),
        compiler_params=_compiler_params(("arbitrary",), 8 * 1024 * 1024),
        name="t5_bias_tiles",
    )(bias_t)


def _dot_nt(a, b, precision=None):
    return lax.dot_general(a, b, (((1,), (1,)), ((), ())), precision=precision,
                           preferred_element_type=jnp.float32)


def _moba_prompt_body(bias_ref, q_ref, k_ref, v_ref, tiles_ref, o_ref, *, n_blk):
    h = pl.program_id(0)
    t = n_blk * MOBA_BLOCK
    n_top = min(MOBA_TOPK, n_blk - 1)
    scale = HEAD_DIM ** -0.5
    q = q_ref[...]
    k = k_ref[...]
    qb = q.astype(jnp.bfloat16)
    kb = k.astype(jnp.bfloat16)
    vb = v_ref[...].astype(jnp.bfloat16)
    own_tile = tiles_ref[0, 0]
    adj_tile = tiles_ref[0, 1]

    if n_top > 0:
        kmean = jnp.mean(k.reshape(n_blk, MOBA_BLOCK, HEAD_DIM), axis=1)
        gate = _dot_nt(kmean, q, precision=lax.Precision.HIGHEST)
        nidx = lax.broadcasted_iota(jnp.int32, (n_blk, t), 0)
        qblk = lax.broadcasted_iota(jnp.int32, (n_blk, t), 1) // MOBA_BLOCK
        rank = jnp.zeros((n_blk, t), jnp.int32)
        for n2 in range(n_blk):
            g2 = gate[n2:n2 + 1, :]
            beats = (n2 < qblk) & ((g2 > gate) | ((g2 == gate) & (n2 < nidx)))
            rank = rank + beats.astype(jnp.int32)
        sel_t = ((nidx < qblk) & (rank < n_top)).astype(jnp.float32)
        pad = jnp.zeros((LANES - n_blk, t), jnp.float32)
        sel = jnp.concatenate([sel_t, pad], axis=0).T.astype(jnp.bfloat16)
        expand = (lax.broadcasted_iota(jnp.int32, (LANES, t), 0)
                  == lax.broadcasted_iota(jnp.int32, (LANES, t), 1) // MOBA_BLOCK
                  ).astype(jnp.bfloat16)
        far_bias = bias_ref[h, NUM_BUCKETS - 1]

    for j in range(n_blk):
        r0 = j * MOBA_BLOCK
        nk = r0 + MOBA_BLOCK
        s = _dot_nt(qb[r0:nk], kb[:nk]) * scale
        if j == 0 or n_top == 0:
            s = s[:, r0:nk] + own_tile
            vals = vb[r0:nk]
        else:
            picked = jnp.dot(sel[r0:nk], expand[:, :r0],
                             preferred_element_type=jnp.float32) > 0.5
            near = jnp.where(picked[:, r0 - MOBA_BLOCK:], adj_tile, NEG_INF)
            parts = [near, own_tile]
            if j > 1:
                parts.insert(0, jnp.where(picked[:, :r0 - MOBA_BLOCK], far_bias, NEG_INF))
            s = s + jnp.concatenate(parts, axis=1)
            vals = vb[:nk]
        m = jnp.max(s, axis=1, keepdims=True)
        p = jnp.exp(s - m)
        l = jnp.sum(p, axis=1, keepdims=True)
        o = jnp.dot(p.astype(jnp.bfloat16), vals, preferred_element_type=jnp.float32)
        o_ref[r0:nk, :] = (o / l).astype(o_ref.dtype)


def _moba_prompt(q, k, v, bias_t, tiles, *, batch, seq):
    m, width = q.shape
    heads = width // HEAD_DIM
    assert seq % MOBA_BLOCK == 0 and m == batch * seq
    n_blk = seq // MOBA_BLOCK
    spec = pl.BlockSpec((seq, HEAD_DIM), lambda h, b, bias: (b, h))
    vmem = 2 * (3 * seq * HEAD_DIM * 4 + seq * HEAD_DIM * 2) + 8 * MOBA_BLOCK * seq * 4
    return pl.pallas_call(
        functools.partial(_moba_prompt_body, n_blk=n_blk),
        out_shape=jax.ShapeDtypeStruct((m, width), jnp.bfloat16),
        grid_spec=pltpu.PrefetchScalarGridSpec(
            num_scalar_prefetch=1,
            grid=(heads, batch),
            in_specs=[spec, spec, spec,
                      pl.BlockSpec((1, 2, MOBA_BLOCK, MOBA_BLOCK),
                                   lambda h, b, bias: (h, 0, 0, 0))],
            out_specs=spec,
        ),
        compiler_params=_compiler_params(("parallel", "parallel"), vmem + 8 * 1024 * 1024),
        name="moba_prompt",
    )(bias_t, q, k, v, tiles)


def _page_mean_body(pt_ref, k_ref, o_ref, *, ppb):
    n = pl.program_id(2)
    p = pl.program_id(3)
    part = jnp.sum(k_ref[0, 0], axis=0, keepdims=True)

    @pl.when(p == 0)
    def _():
        o_ref[0, 0, pl.ds(n, 1), :] = part

    @pl.when(p > 0)
    def _():
        o_ref[0, 0, pl.ds(n, 1), :] += part

    @pl.when(p == ppb - 1)
    def _():
        o_ref[0, 0, pl.ds(n, 1), :] *= 1.0 / MOBA_BLOCK


def _page_block_means(cache_k, page_table):
    depth, _, page, width = cache_k.shape
    db, n_pages = page_table.shape
    ppb = MOBA_BLOCK // PAGE_SIZE
    assert page == PAGE_SIZE and n_pages % ppb == 0
    n_full = n_pages // ppb
    return pl.pallas_call(
        functools.partial(_page_mean_body, ppb=ppb),
        out_shape=jax.ShapeDtypeStruct((depth, db, n_full, width), jnp.float32),
        grid_spec=pltpu.PrefetchScalarGridSpec(
            num_scalar_prefetch=1,
            grid=(depth, db, n_full, ppb),
            in_specs=[pl.BlockSpec((1, 1, page, width),
                                   lambda l, b, n, p, pt: (l, pt[b, n * ppb + p], 0, 0))],
            out_specs=pl.BlockSpec((1, 1, n_full, width), lambda l, b, n, p, pt: (l, b, 0, 0)),
        ),
        compiler_params=_compiler_params(("arbitrary",) * 4, 8 * page * width * 4),
        name="page_block_means",
    )(page_table, cache_k)


def _decode_topk_body(q_ref, km_ref, o_ref, *, heads, n_top):
    tn = q_ref.shape[1]
    n_full = km_ref.shape[1]
    lane = lax.broadcasted_iota(jnp.int32, (tn, LANES), 1)
    col = lax.broadcasted_iota(jnp.int32, (tn, n_full), 1).astype(jnp.float32)
    for h in range(heads):
        c0 = h * HEAD_DIM
        gate = _dot_nt(q_ref[0, :, c0:c0 + HEAD_DIM], km_ref[0, :, c0:c0 + HEAD_DIM],
                       precision=lax.Precision.HIGHEST)
        out = jnp.zeros((tn, LANES), jnp.int32)
        for r in range(n_top):
            best = jnp.max(gate, axis=1, keepdims=True)
            idx = jnp.min(jnp.where(gate == best, col, float(n_full)), axis=1, keepdims=True)
            out = jnp.where(lane == r, idx.astype(jnp.int32), out)
            gate = jnp.where(col == idx, NEG_INF, gate)
        o_ref[0, h] = out


def _decode_topk(q, kmean, *, heads, n_top):
    db, tn, width = q.shape
    n_full = kmean.shape[1]
    return pl.pallas_call(
        functools.partial(_decode_topk_body, heads=heads, n_top=n_top),
        out_shape=jax.ShapeDtypeStruct((db, heads, tn, LANES), jnp.int32),
        grid=(db,),
        in_specs=[pl.BlockSpec((1, tn, width), lambda b: (b, 0, 0)),
                  pl.BlockSpec((1, n_full, width), lambda b: (b, 0, 0))],
        out_specs=pl.BlockSpec((1, heads, tn, LANES), lambda b: (b, 0, 0, 0)),
        compiler_params=_compiler_params(("parallel",), 16 * 1024 * 1024),
        name="decode_topk",
    )(q, kmean)


def _moba_decode_body(top_ref, pt_ref, bias_ref, q_ref, k_ref, v_ref, ck_ref, cv_ref, o_ref,
                      kbuf, vbuf, sems, *, layer, heads, tn, n_top, past):
    step = pl.program_id(0)
    n_steps = pl.num_programs(0)
    ppb = MOBA_BLOCK // PAGE_SIZE
    n_sel = tn * n_top * MOBA_BLOCK
    own_rows = LANES

    def gather(s, slot, start):
        b = s // heads
        h = s % heads
        for t in range(tn):
            for r in range(n_top):
                blk = top_ref[((b * heads + h) * tn + t) * n_top + r]
                for p in range(ppb):
                    phys = pt_ref[b, blk * ppb + p]
                    row0 = (t * n_top + r) * MOBA_BLOCK + p * PAGE_SIZE
                    for src, dst, sem in ((ck_ref, kbuf, sems.at[slot, 0]),
                                          (cv_ref, vbuf, sems.at[slot, 1])):
                        cp = pltpu.make_async_copy(
                            src.at[layer, phys, :,
                                   pl.ds(pl.multiple_of(h * HEAD_DIM, HEAD_DIM), HEAD_DIM)],
                            dst.at[slot, pl.ds(row0, PAGE_SIZE), :], sem)
                        if start:
                            cp.start()
                        else:
                            cp.wait()

    slot = step % 2

    @pl.when(step == 0)
    def _():
        gather(step, 0, True)

    @pl.when(step + 1 < n_steps)
    def _():
        gather(step + 1, 1 - slot, True)

    gather(step, slot, False)

    b = step // heads
    h = step % heads
    scale = HEAD_DIM ** -0.5
    tp = q_ref.shape[1]
    qb = q_ref[0].astype(jnp.bfloat16)
    zeros = jnp.zeros((own_rows - tp, HEAD_DIM), jnp.float32)
    k_all = jnp.concatenate([kbuf[slot], k_ref[0], zeros], axis=0).astype(jnp.bfloat16)
    v_all = jnp.concatenate([vbuf[slot], v_ref[0], zeros], axis=0).astype(jnp.bfloat16)
    n_cols = n_sel + own_rows
    s = _dot_nt(qb, k_all) * scale

    row = lax.broadcasted_iota(jnp.int32, (tp, n_cols), 0)
    col = lax.broadcasted_iota(jnp.int32, (tp, n_cols), 1)
    seg = col // MOBA_BLOCK
    key_pos = past + (col - n_sel)
    for t in range(tn):
        for r in range(n_top):
            blk = top_ref[((b * heads + h) * tn + t) * n_top + r]
            key_pos = jnp.where(seg == t * n_top + r,
                                blk * MOBA_BLOCK + col % MOBA_BLOCK, key_pos)
    q_pos = past + row
    own = col >= n_sel
    valid = (own & ((col - n_sel) <= row)) | (jnp.logical_not(own) & (seg // n_top == row))
    bias = _t5_bias(jnp.maximum(q_pos - key_pos, 0), bias_ref, h)
    s = jnp.where(valid, s + bias, NEG_INF)
    m = jnp.max(s, axis=1, keepdims=True)
    p = jnp.exp(s - m)
    l = jnp.sum(p, axis=1, keepdims=True)
    o = jnp.dot(p.astype(jnp.bfloat16), v_all, preferred_element_type=jnp.float32)
    o_ref[0] = (o / l).astype(o_ref.dtype)


def _moba_decode(q, k, v, cache_k, cache_v, page_table, top_idx, bias_t, *, layer, heads):
    db, tp, width = q.shape
    tn, n_top = top_idx.shape[-2:]
    past = page_table.shape[1] * PAGE_SIZE
    n_sel = tn * n_top * MOBA_BLOCK
    spec = pl.BlockSpec((1, tp, HEAD_DIM), lambda s, *_: (s // heads, 0, s % heads))
    any_spec = pl.BlockSpec(memory_space=pl.ANY)
    vmem = 2 * 2 * n_sel * HEAD_DIM * 4 + 6 * (n_sel + LANES) * HEAD_DIM * 4
    return pl.pallas_call(
        functools.partial(_moba_decode_body, layer=layer, heads=heads, tn=tn, n_top=n_top,
                          past=past),
        out_shape=jax.ShapeDtypeStruct((db, tp, width), jnp.bfloat16),
        grid_spec=pltpu.PrefetchScalarGridSpec(
            num_scalar_prefetch=3,
            grid=(db * heads,),
            in_specs=[spec, spec, spec, any_spec, any_spec],
            out_specs=spec,
            scratch_shapes=[pltpu.VMEM((2, n_sel, HEAD_DIM), jnp.float32),
                            pltpu.VMEM((2, n_sel, HEAD_DIM), jnp.float32),
                            pltpu.SemaphoreType.DMA((2, 2))],
        ),
        compiler_params=_compiler_params(("arbitrary",), vmem + 8 * 1024 * 1024),
        name="moba_decode",
    )(top_idx.reshape(-1), page_table, bias_t, q, k, v, cache_k, cache_v)


def _hgrn2_gates(qb, fb, layer, lbs):
    q = qb * jax.nn.sigmoid(qb)
    if layer == 0:
        logf = jnp.minimum(fb, 0.0) - jnp.log1p(jnp.exp(-jnp.abs(fb)))
        k = jax.nn.sigmoid(-fb)
    else:
        e = jnp.exp(lbs - jnp.max(lbs, axis=0, keepdims=True))
        p = e / jnp.sum(e, axis=0, keepdims=True)
        lb = jnp.sum(p[:layer + 1], axis=0, keepdims=True) - p[0:1]
        logf = jnp.log(lb + (1.0 - lb) * jax.nn.sigmoid(fb))
        k = (1.0 - lb) * jax.nn.sigmoid(-fb)
    return q, logf, k


def _gla_chunk(q, k, v, g, st):
    c = q.shape[0]
    row = lax.broadcasted_iota(jnp.int32, q.shape, 0)
    b = g
    sh = 1
    while sh < c:
        b = b + jnp.where(row >= sh, pltpu.roll(b, sh, 0), 0.0)
        sh *= 2

    o = jnp.sum(q * k, axis=1, keepdims=True) * v
    for d in range(1, SUBLANES):
        ok = (row % SUBLANES) >= d
        decay = jnp.exp(jnp.where(ok, b - pltpu.roll(b, d, 0), NEG_INF))
        w = jnp.sum(q * decay * pltpu.roll(k, d, 0), axis=1, keepdims=True)
        o = o + w * pltpu.roll(v, d, 0)

    if c > SUBLANES:
        scores = jnp.zeros((c, c), jnp.float32)
        r2 = lax.broadcasted_iota(jnp.int32, (c, c), 0)
        c2 = lax.broadcasted_iota(jnp.int32, (c, c), 1)
        s = 2 * SUBLANES
        while s <= c:
            half = s // 2
            b3 = b.reshape(c // s, s, b.shape[1])
            ref = jnp.broadcast_to(b3[:, half - 1:half, :], b3.shape).reshape(b.shape)
            upper = (row % s) >= half
            x = jnp.where(upper, q, k) * jnp.exp(jnp.where(upper, b - ref, ref - b))
            qs = jnp.where(upper, x, 0.0).astype(jnp.bfloat16)
            ks = jnp.where(upper, 0.0, x).astype(jnp.bfloat16)
            part = _dot_nt(qs, ks)
            scores = scores + (part if s == c else jnp.where(r2 // s == c2 // s, part, 0.0))
            s *= 2
        o = o + jnp.dot(scores.astype(jnp.bfloat16), v.astype(jnp.bfloat16),
                        preferred_element_type=jnp.float32)

    b_last = b[c - 1:c, :]
    stb = st.astype(jnp.bfloat16)
    o = o + _dot_nt((q * jnp.exp(b)).astype(jnp.bfloat16), stb)
    kw = (k * jnp.exp(b_last - b)).astype(jnp.bfloat16)
    st_new = st * jnp.exp(b_last) + lax.dot_general(
        v.astype(jnp.bfloat16), kw, (((0,), (0,)), ((), ())), preferred_element_type=jnp.float32)
    return o, st_new


def _hgrn2_out(o, gate, gain):
    ms = jnp.mean(o * o, axis=-1, keepdims=True)
    return o * lax.rsqrt(ms + EPS) * gain * (gate * jax.nn.sigmoid(gate))


def _hgrn2_prompt_body(qb_ref, fb_ref, ib_ref, gb_ref, lb_ref, gain_ref, o_ref, s_ref, st_ref,
                       *, layer, n_chunks):
    st_ref[...] = jnp.zeros_like(st_ref)
    lbs = lb_ref[...]
    gain = gain_ref[layer:layer + 1, :]

    def chunk(i, carry):
        rows = pl.ds(pl.multiple_of(i * GLA_CHUNK, GLA_CHUNK), GLA_CHUNK)
        q, logf, k = _hgrn2_gates(qb_ref[rows, :], fb_ref[rows, :], layer, lbs)
        o, st = _gla_chunk(q, k, ib_ref[rows, :], logf, st_ref[...])
        st_ref[...] = st
        o_ref[rows, :] = _hgrn2_out(o, gb_ref[rows, :], gain).astype(o_ref.dtype)
        return carry

    lax.fori_loop(0, n_chunks, chunk, 0)
    s_ref[0, 0] = st_ref[...].T


def _hgrn2_prompt(proj, lower_logits, gain, *, layer, batch, seq, heads):
    m = proj.shape[0]
    assert seq % GLA_CHUNK == 0 and m == batch * seq
    depth = lower_logits.shape[0]

    def col(j):
        return pl.BlockSpec((seq, HG_K), lambda b, h: (b, j * heads + h))

    par = pl.BlockSpec((depth, HG_K), lambda b, h: (0, h))
    gspec = pl.BlockSpec((depth, HG_V), lambda b, h: (0, 0))
    vmem = 2 * (4 * seq * HG_K * 4 + seq * HG_V * 2) + 16 * GLA_CHUNK * GLA_CHUNK * 4
    return pl.pallas_call(
        functools.partial(_hgrn2_prompt_body, layer=layer, n_chunks=seq // GLA_CHUNK),
        out_shape=(jax.ShapeDtypeStruct((m, heads * HG_V), jnp.bfloat16),
                   jax.ShapeDtypeStruct((batch, heads, HG_K, HG_V), jnp.float32)),
        grid=(batch, heads),
        in_specs=[col(0), col(1), col(2), col(3), par, gspec],
        out_specs=(pl.BlockSpec((seq, HG_V), lambda b, h: (b, h)),
                   pl.BlockSpec((1, 1, HG_K, HG_V), lambda b, h: (b, h, 0, 0))),
        scratch_shapes=[pltpu.VMEM((HG_V, HG_K), jnp.float32)],
        compiler_params=_compiler_params(("parallel", "parallel"), vmem + 8 * 1024 * 1024),
        name="hgrn2_prompt",
    )(proj, proj, proj, proj, lower_logits, gain)


def _hgrn2_decode_body(qb_ref, fb_ref, ib_ref, gb_ref, s0_ref, lb_ref, gain_ref, o_ref, s_ref,
                       *, layer, tn):
    keep = lax.broadcasted_iota(jnp.int32, (SUBLANES, HG_K), 0) < tn
    q, logf, k = _hgrn2_gates(qb_ref[0], fb_ref[0], layer, lb_ref[...])
    logf = jnp.where(keep, logf, 0.0)
    k = jnp.where(keep, k, 0.0)
    o, st = _gla_chunk(q, k, ib_ref[0], logf, s0_ref[0, 0].T)
    s_ref[0, 0] = st.T
    o_ref[0] = _hgrn2_out(o, gb_ref[0], gain_ref[layer:layer + 1, :]).astype(o_ref.dtype)


def _hgrn2_decode(proj, s0, lower_logits, gain, *, layer, heads, tn):
    db, tp, _ = proj.shape
    assert tn <= tp == SUBLANES
    depth = lower_logits.shape[0]

    def col(j):
        return pl.BlockSpec((1, tp, HG_K), lambda b, h: (b, 0, j * heads + h))

    sspec = pl.BlockSpec((1, 1, HG_K, HG_V), lambda b, h: (b, h, 0, 0))
    return pl.pallas_call(
        functools.partial(_hgrn2_decode_body, layer=layer, tn=tn),
        out_shape=(jax.ShapeDtypeStruct((db, tp, heads * HG_V), jnp.bfloat16),
                   jax.ShapeDtypeStruct(s0.shape, jnp.float32)),
        grid=(db, heads),
        in_specs=[col(0), col(1), col(2), col(3), sspec,
                  pl.BlockSpec((depth, HG_K), lambda b, h: (0, h)),
                  pl.BlockSpec((depth, HG_V), lambda b, h: (0, 0))],
        out_specs=(pl.BlockSpec((1, tp, HG_V), lambda b, h: (b, 0, h)), sspec),
        compiler_params=_compiler_params(("parallel", "parallel"), 16 * 1024 * 1024),
        name="hgrn2_decode",
    )(proj, proj, proj, proj, s0, lower_logits, gain)


def _trunk_layer(x, mixers, weights, *, a_width, b_width):
    w_in, w_ba, w_bb, w_out, w_up, w_down, g_mix, g_mlp = weights
    d = x.shape[1]
    h = _rmsnorm(x, g_mix, jnp.bfloat16)
    qa = _matmul(h, w_in, col0=0, ncols=a_width, out_dtype=jnp.float32, name="proj_q")
    ka = _matmul(h, w_in, col0=a_width, ncols=a_width, out_dtype=jnp.float32, name="proj_k")
    va = _matmul(h, w_in, col0=2 * a_width, ncols=a_width, out_dtype=jnp.float32, name="proj_v")
    hg = _matmul(h, w_in, col0=3 * a_width, ncols=4 * b_width, out_dtype=jnp.float32,
                 name="proj_hgrn")
    gates = _matmul(h, w_in, col0=3 * a_width + 4 * b_width, ncols=2 * d,
                    out_dtype=jnp.bfloat16, act=_act_sigmoid, name="proj_gates")
    o_a = mixers["moba"](qa, ka, va)
    o_b, s_new = mixers["hgrn2"](hg)
    merged = _merge_branches(o_a, o_b, w_ba, w_bb, gates)
    x = _matmul(merged, w_out, out_dtype=jnp.float32, residual=x, name="out_proj")
    h2 = _rmsnorm(x, g_mlp, jnp.bfloat16)
    u = _matmul(h2, w_up, out_dtype=jnp.bfloat16, act=_act_relu2, name="ffn_up")
    x = _matmul(u, w_down, out_dtype=jnp.float32, residual=x, name="ffn_down")
    return x, ka, va, s_new


def kernel(x_prompt, x_sample, cache_k, cache_v, state_hgrn, page_table, w_in, w_branch_a,
           w_branch_b, w_out, w_up, w_down, norm_mix, norm_mlp, hg_norm, hg_lower_bounds,
           rel_bias, final_norm):
    batch, seq, d = x_prompt.shape
    db, tn, _ = x_sample.shape
    depth = w_in.shape[0]
    a_width = w_branch_a.shape[1]
    b_width = w_branch_b.shape[1]
    a_heads = a_width // HEAD_DIM
    b_heads = b_width // HG_V
    n_pages = page_table.shape[1]
    n_full = (n_pages * PAGE_SIZE) // MOBA_BLOCK
    assert (n_pages * PAGE_SIZE) % MOBA_BLOCK == 0
    n_top = min(MOBA_TOPK, n_full)
    assert n_top > 0

    bf16 = jnp.bfloat16
    bias_t = rel_bias.astype(jnp.float32).T
    tiles = _bias_tiles(bias_t)
    ck = cache_k.reshape(cache_k.shape[0], cache_k.shape[1], PAGE_SIZE, a_width)
    cv = cache_v.reshape(cache_v.shape[0], cache_v.shape[1], PAGE_SIZE, a_width)
    kmeans = _page_block_means(ck, page_table)

    xp = x_prompt.reshape(batch * seq, d)
    xs = x_sample.reshape(db * tn, d)
    outs = {name: [] for name in ("kp", "vp", "sp", "ks", "vs", "ss")}
    for l in range(depth):
        weights = (w_in[l].astype(bf16), w_branch_a[l].astype(bf16), w_branch_b[l].astype(bf16),
                   w_out[l].astype(bf16), w_up[l].astype(bf16), w_down[l].astype(bf16),
                   norm_mix[l], norm_mlp[l])

        def moba_p(q, k, v):
            return _moba_prompt(q, k, v, bias_t, tiles, batch=batch, seq=seq)

        def hgrn_p(hg, l=l):
            return _hgrn2_prompt(hg, hg_lower_bounds, hg_norm, layer=l, batch=batch, seq=seq,
                                 heads=b_heads)

        def pad_tokens(a):
            return jnp.pad(a.reshape(db, tn, -1), ((0, 0), (0, SUBLANES - tn), (0, 0)))

        def moba_s(q, k, v, l=l):
            q3, k3, v3 = pad_tokens(q), pad_tokens(k), pad_tokens(v)
            top = _decode_topk(q3, kmeans[l], heads=a_heads, n_top=n_top)[:, :, :tn, :n_top]
            o = _moba_decode(q3, k3, v3, ck, cv, page_table, top, bias_t, layer=l, heads=a_heads)
            return o[:, :tn].reshape(db * tn, a_width)

        def hgrn_s(hg, l=l):
            o, s = _hgrn2_decode(pad_tokens(hg), state_hgrn[l], hg_lower_bounds, hg_norm,
                                 layer=l, heads=b_heads, tn=tn)
            return o[:, :tn].reshape(db * tn, b_width), s

        xp, kp, vp, sp = _trunk_layer(xp, {"moba": moba_p, "hgrn2": hgrn_p}, weights,
                                      a_width=a_width, b_width=b_width)
        xs, ks, vs, ss = _trunk_layer(xs, {"moba": moba_s, "hgrn2": hgrn_s}, weights,
                                      a_width=a_width, b_width=b_width)
        outs["kp"].append(kp.reshape(batch, seq, a_heads, HEAD_DIM))
        outs["vp"].append(vp.reshape(batch, seq, a_heads, HEAD_DIM))
        outs["sp"].append(sp)
        outs["ks"].append(ks.reshape(db, tn, a_heads, HEAD_DIM))
        outs["vs"].append(vs.reshape(db, tn, a_heads, HEAD_DIM))
        outs["ss"].append(ss)

    y_prompt = _rmsnorm(xp, final_norm, jnp.float32).reshape(batch, seq, d)
    y_sample = _rmsnorm(xs, final_norm, jnp.float32).reshape(db, tn, d)
    return (y_prompt, y_sample, jnp.stack(outs["kp"]), jnp.stack(outs["vp"]),
            jnp.stack(outs["sp"]), jnp.stack(outs["ks"]), jnp.stack(outs["vs"]),
            jnp.stack(outs["ss"]))
```

```python
import functools
import math

import jax
import jax.numpy as jnp
from jax import lax
from jax.experimental import pallas as pl
from jax.experimental.pallas import tpu as pltpu

HEAD_DIM = 128
MOBA_BLOCK = 256
MOBA_TOPK = 3
NUM_BUCKETS = 32
MAX_DISTANCE = 128
PAGE_SIZE = 128
HG_K = 128
HG_V = 128
EPS = 1e-6

V7X_VMEM_LIMIT_BYTES = 60000 * 1024
SUBLANES = 8
LANES = 128

GLA_CHUNK = 256
GLA_GROUP = 4
LOG2_E = math.log2(math.e)
NEG_INF = float("-inf")


def _compiler_params(semantics, vmem_bytes):
    limit = int(min(max(vmem_bytes, 16 * 1024 * 1024), V7X_VMEM_LIMIT_BYTES))
    return pltpu.CompilerParams(dimension_semantics=semantics, vmem_limit_bytes=limit)


def _rmsnorm_body(x_ref, g_ref, o_ref):
    x = x_ref[...]
    ms = jnp.mean(x * x, axis=-1, keepdims=True)
    o_ref[...] = (x * lax.rsqrt(ms + EPS) * g_ref[...]).astype(o_ref.dtype)


def _rmsnorm(x, gain, out_dtype):
    m, d = x.shape
    tm = min(m, 256)
    assert m % tm == 0
    block_bytes = tm * d * (4 + jnp.dtype(out_dtype).itemsize)
    return pl.pallas_call(
        _rmsnorm_body,
        out_shape=jax.ShapeDtypeStruct((m, d), out_dtype),
        grid=(m // tm,),
        in_specs=[pl.BlockSpec((tm, d), lambda i: (i, 0)),
                  pl.BlockSpec((1, d), lambda i: (0, 0))],
        out_specs=pl.BlockSpec((tm, d), lambda i: (i, 0)),
        compiler_params=_compiler_params(("parallel",), 4 * block_bytes),
        name="rmsnorm",
    )(x, gain.reshape(1, d))


def _act_none(a):
    return a


def _act_sigmoid(a):
    return jax.nn.sigmoid(a)


def _act_relu2(a):
    return jnp.square(jnp.maximum(a, 0.0))


def _mm_body(xp_ref, xs_ref, w_ref, *refs, act, has_res, nk):
    if has_res:
        rp_ref, rs_ref, op_ref, os_ref = refs
    else:
        op_ref, os_ref = refs
        rp_ref = rs_ref = None
    i = pl.program_id(1)
    k = pl.program_id(2)

    def emit(x_ref, r_ref, o_ref):
        acc = jnp.dot(x_ref[...], w_ref[...], preferred_element_type=jnp.float32)
        if nk == 1:
            if has_res:
                acc = acc + r_ref[...]
            o_ref[...] = act(acc).astype(o_ref.dtype)
        else:
            @pl.when(k == 0)
            def _():
                o_ref[...] = (acc + r_ref[...]) if has_res else acc

            @pl.when(k > 0)
            def _():
                o_ref[...] += acc

    emit(xp_ref, rp_ref, op_ref)

    @pl.when(i == 0)
    def _():
        emit(xs_ref, rs_ref, os_ref)


def _matmul(xp, xs, w, *, layer, col0=0, ncols=None, out_dtype, act=_act_none, residual=None,
            name):
    m, kdim = xp.shape
    ms = xs.shape[0]
    ncols = w.shape[2] if ncols is None else ncols
    tm = min(m, 1024)
    tk = kdim if kdim <= 4096 else 2048
    tn = math.gcd(ncols, col0, 1024)
    assert m % tm == 0 and kdim % tk == 0 and ncols % tn == 0 and col0 % tn == 0
    nm, nn, nk = m // tm, ncols // tn, kdim // tk
    assert nk == 1 or (out_dtype == jnp.float32 and act is _act_none)
    c0 = col0 // tn
    in_specs = [pl.BlockSpec((tm, tk), lambda n, i, k: (i, k)),
                pl.BlockSpec((ms, tk), lambda n, i, k: (0, k)),
                pl.BlockSpec((None, tk, tn), lambda n, i, k: (layer, k, n + c0))]
    args = [xp, xs, w]
    osize = jnp.dtype(out_dtype).itemsize
    vmem = (2 * ((tm + ms) * tk * 2 + tk * tn * 2 + (tm + ms) * tn * osize)
            + 2 * (tm + ms) * tn * 4)
    if residual is not None:
        in_specs += [pl.BlockSpec((tm, tn), lambda n, i, k: (i, n)),
                     pl.BlockSpec((ms, tn), lambda n, i, k: (0, n))]
        args += list(residual)
        vmem += 2 * (tm + ms) * tn * 4
    return pl.pallas_call(
        functools.partial(_mm_body, act=act, has_res=residual is not None, nk=nk),
        out_shape=(jax.ShapeDtypeStruct((m, ncols), out_dtype),
                   jax.ShapeDtypeStruct((ms, ncols), out_dtype)),
        grid=(nn, nm, nk),
        in_specs=in_specs,
        out_specs=(pl.BlockSpec((tm, tn), lambda n, i, k: (i, n)),
                   pl.BlockSpec((ms, tn), lambda n, i, k: (0, n))),
        compiler_params=_compiler_params(("arbitrary", "arbitrary", "arbitrary"),
                                         vmem + 4 * 1024 * 1024),
        name=name,
    )(*args)


def _merge_body(oap_ref, obp_ref, oas_ref, obs_ref, wa_ref, wb_ref, gap_ref, gbp_ref, gas_ref,
                gbs_ref, op_ref, os_ref):
    def emit(oa_ref, ob_ref, ga_ref, gb_ref, o_ref):
        a = jnp.dot(oa_ref[...], wa_ref[...], preferred_element_type=jnp.float32)
        b = jnp.dot(ob_ref[...], wb_ref[...], preferred_element_type=jnp.float32)
        o_ref[...] = (ga_ref[...].astype(jnp.float32) * a
                      + gb_ref[...].astype(jnp.float32) * b).astype(o_ref.dtype)

    emit(oap_ref, obp_ref, gap_ref, gbp_ref, op_ref)

    @pl.when(pl.program_id(1) == 0)
    def _():
        emit(oas_ref, obs_ref, gas_ref, gbs_ref, os_ref)


def _merge_branches(o_a, o_b, w_ba, w_bb, gates, *, layer):
    (oap, oas), (obp, obs), (gp, gs) = o_a, o_b, gates
    m, ka = oap.shape
    ms = oas.shape[0]
    kb = obp.shape[1]
    d = w_ba.shape[2]
    tm = min(m, 1024)
    tn = min(d, 512)
    assert m % tm == 0 and d % tn == 0
    goff = d // tn
    rows = tm + ms
    vmem = 2 * 2 * (rows * ka + rows * kb + ka * tn + kb * tn + 3 * rows * tn) + 3 * rows * tn * 4
    return pl.pallas_call(
        _merge_body,
        out_shape=(jax.ShapeDtypeStruct((m, d), jnp.bfloat16),
                   jax.ShapeDtypeStruct((ms, d), jnp.bfloat16)),
        grid=(d // tn, m // tm),
        in_specs=[pl.BlockSpec((tm, ka), lambda n, i: (i, 0)),
                  pl.BlockSpec((tm, kb), lambda n, i: (i, 0)),
                  pl.BlockSpec((ms, ka), lambda n, i: (0, 0)),
                  pl.BlockSpec((ms, kb), lambda n, i: (0, 0)),
                  pl.BlockSpec((None, ka, tn), lambda n, i: (layer, 0, n)),
                  pl.BlockSpec((None, kb, tn), lambda n, i: (layer, 0, n)),
                  pl.BlockSpec((tm, tn), lambda n, i: (i, n)),
                  pl.BlockSpec((tm, tn), lambda n, i: (i, n + goff)),
                  pl.BlockSpec((ms, tn), lambda n, i: (0, n)),
                  pl.BlockSpec((ms, tn), lambda n, i: (0, n + goff))],
        out_specs=(pl.BlockSpec((tm, tn), lambda n, i: (i, n)),
                   pl.BlockSpec((ms, tn), lambda n, i: (0, n))),
        compiler_params=_compiler_params(("arbitrary", "arbitrary"), vmem + 4 * 1024 * 1024),
        name="merge_branches",
    )(oap, obp, oas, obs, w_ba, w_bb, gp, gp, gs, gs)


def _t5_bias(dist, bias_ref, head):
    max_exact = NUM_BUCKETS // 2
    nf = jnp.maximum(dist, 1).astype(jnp.float32)
    far = max_exact + (jnp.log(nf / max_exact) / math.log(MAX_DISTANCE / max_exact)
                       * (NUM_BUCKETS - max_exact)).astype(jnp.int32)
    far = jnp.minimum(far, NUM_BUCKETS - 1)
    bucket = jnp.where(dist < max_exact, dist, far)
    out = jnp.zeros(dist.shape, jnp.float32)
    for k in range(NUM_BUCKETS):
        out = jnp.where(bucket == k, bias_ref[head, k], out)
    return out


def _bias_tiles_body(bias_ref, o_ref):
    h = pl.program_id(0)
    row = lax.broadcasted_iota(jnp.int32, (MOBA_BLOCK, MOBA_BLOCK), 0)
    col = lax.broadcasted_iota(jnp.int32, (MOBA_BLOCK, MOBA_BLOCK), 1)
    own = _t5_bias(jnp.maximum(row - col, 0), bias_ref, h)
    o_ref[0, 0] = jnp.where(row >= col, own, NEG_INF)
    o_ref[0, 1] = _t5_bias(row - col + MOBA_BLOCK, bias_ref, h)


def _bias_tiles(bias_t):
    h = bias_t.shape[0]
    return pl.pallas_call(
        _bias_tiles_body,
        out_shape=jax.ShapeDtypeStruct((h, 2, MOBA_BLOCK, MOBA_BLOCK), jnp.float32),
        grid_spec=pltpu.PrefetchScalarGridSpec(
            num_scalar_prefetch=1,
            grid=(h,),
            in_specs=[],
            out_specs=pl.BlockSpec((1, 2, MOBA_BLOCK, MOBA_BLOCK), lambda i, b: (i, 0, 0, 0)),
        ),
        compiler_params=_compiler_params(("arbitrary",), 8 * 1024 * 1024),
        name="t5_bias_tiles",
    )(bias_t)


def _dot_nt(a, b, precision=None):
    return lax.dot_general(a, b, (((1,), (1,)), ((), ())), precision=precision,
                           preferred_element_type=jnp.float32)


def _moba_prompt_body(bias_ref, q_ref, k_ref, v_ref, tiles_ref, o_ref, *, n_blk):
    h = pl.program_id(0)
    t = n_blk * MOBA_BLOCK
    n_top = min(MOBA_TOPK, n_blk - 1)
    scale = HEAD_DIM ** -0.5
    q = q_ref[...]
    k = k_ref[...]
    qb = q.astype(jnp.bfloat16)
    kb = k.astype(jnp.bfloat16)
    vb = v_ref[...].astype(jnp.bfloat16)
    own_tile = tiles_ref[0, 0]
    adj_tile = tiles_ref[0, 1]

    if n_top > 0:
        kmean = jnp.mean(k.reshape(n_blk, MOBA_BLOCK, HEAD_DIM), axis=1)
        gate = _dot_nt(kmean, q, precision=lax.Precision.HIGHEST)
        nidx = lax.broadcasted_iota(jnp.int32, (n_blk, t), 0)
        qblk = lax.broadcasted_iota(jnp.int32, (n_blk, t), 1) // MOBA_BLOCK
        rank = jnp.zeros((n_blk, t), jnp.int32)
        for n2 in range(n_blk):
            g2 = gate[n2:n2 + 1, :]
            beats = (n2 < qblk) & ((g2 > gate) | ((g2 == gate) & (n2 < nidx)))
            rank = rank + beats.astype(jnp.int32)
        sel_t = ((nidx < qblk) & (rank < n_top)).astype(jnp.float32)
        pad = jnp.zeros((LANES - n_blk, t), jnp.float32)
        sel = jnp.concatenate([sel_t, pad], axis=0).T.astype(jnp.bfloat16)
        expand = (lax.broadcasted_iota(jnp.int32, (LANES, t), 0)
                  == lax.broadcasted_iota(jnp.int32, (LANES, t), 1) // MOBA_BLOCK
                  ).astype(jnp.bfloat16)
        far_bias = bias_ref[h, NUM_BUCKETS - 1]

    for j in range(n_blk):
        r0 = j * MOBA_BLOCK
        nk = r0 + MOBA_BLOCK
        s = _dot_nt(qb[r0:nk], kb[:nk]) * scale
        if j == 0 or n_top == 0:
            s = s[:, r0:nk] + own_tile
            vals = vb[r0:nk]
        else:
            picked = jnp.dot(sel[r0:nk], expand[:, :r0],
                             preferred_element_type=jnp.float32) > 0.5
            near = jnp.where(picked[:, r0 - MOBA_BLOCK:], adj_tile, NEG_INF)
            parts = [near, own_tile]
            if j > 1:
                parts.insert(0, jnp.where(picked[:, :r0 - MOBA_BLOCK], far_bias, NEG_INF))
            s = s + jnp.concatenate(parts, axis=1)
            vals = vb[:nk]
        m = jnp.max(s, axis=1, keepdims=True)
        p = jnp.exp(s - m)
        l = jnp.sum(p, axis=1, keepdims=True)
        o = jnp.dot(p.astype(jnp.bfloat16), vals, preferred_element_type=jnp.float32)
        o_ref[r0:nk, :] = (o / l).astype(o_ref.dtype)


def _moba_prompt(q, k, v, bias_t, tiles, *, batch, seq):
    m, width = q.shape
    heads = width // HEAD_DIM
    assert seq % MOBA_BLOCK == 0 and m == batch * seq
    n_blk = seq // MOBA_BLOCK
    spec = pl.BlockSpec((seq, HEAD_DIM), lambda h, b, bias: (b, h))
    vmem = 2 * (3 * seq * HEAD_DIM * 4 + seq * HEAD_DIM * 2) + 8 * MOBA_BLOCK * seq * 4
    return pl.pallas_call(
        functools.partial(_moba_prompt_body, n_blk=n_blk),
        out_shape=jax.ShapeDtypeStruct((m, width), jnp.bfloat16),
        grid_spec=pltpu.PrefetchScalarGridSpec(
            num_scalar_prefetch=1,
            grid=(heads, batch),
            in_specs=[spec, spec, spec,
                      pl.BlockSpec((1, 2, MOBA_BLOCK, MOBA_BLOCK),
                                   lambda h, b, bias: (h, 0, 0, 0))],
            out_specs=spec,
        ),
        compiler_params=_compiler_params(("parallel", "parallel"), vmem + 8 * 1024 * 1024),
        name="moba_prompt",
    )(bias_t, q, k, v, tiles)


PAGES_PER_STEP = 8


def _page_mean_body(pt_ref, *refs, ppb):
    o_ref = refs[-1]
    i = pl.program_id(2)
    blocks = (len(refs) - 1) // ppb
    for j in range(blocks):
        total = sum(jnp.sum(refs[j * ppb + p][0, 0], axis=0) for p in range(ppb))
        o_ref[0, 0, i * blocks + j] = total * (1.0 / MOBA_BLOCK)


def _page_block_means(cache_k, page_table):
    depth, _, page, heads, hd = cache_k.shape
    db, n_pages = page_table.shape
    ppb = MOBA_BLOCK // PAGE_SIZE
    assert page == PAGE_SIZE and n_pages % ppb == 0
    n_full = n_pages // ppb
    pages = math.gcd(n_pages, PAGES_PER_STEP)
    assert pages % ppb == 0

    def page_spec(j):
        return pl.BlockSpec((1, 1, page, heads, hd),
                            lambda l, b, i, pt: (l, pt[b, i * pages + j], 0, 0, 0))

    return pl.pallas_call(
        functools.partial(_page_mean_body, ppb=ppb),
        out_shape=jax.ShapeDtypeStruct((depth, db, n_full, heads, hd), jnp.float32),
        grid_spec=pltpu.PrefetchScalarGridSpec(
            num_scalar_prefetch=1,
            grid=(depth, db, n_pages // pages),
            in_specs=[page_spec(j) for j in range(pages)],
            out_specs=pl.BlockSpec((1, 1, n_full, heads, hd), lambda l, b, i, pt: (l, b, 0, 0, 0)),
        ),
        compiler_params=_compiler_params(("arbitrary",) * 3,
                                         (2 * pages + 4) * page * heads * hd * 4),
        name="page_block_means",
    )(page_table, *([cache_k] * pages))


def _decode_topk_body(q_ref, km_ref, o_ref, *, heads, n_top):
    tn = q_ref.shape[1]
    n_full = km_ref.shape[1]
    lane = lax.broadcasted_iota(jnp.int32, (tn, LANES), 1)
    col = lax.broadcasted_iota(jnp.int32, (tn, n_full), 1).astype(jnp.float32)
    for h in range(heads):
        c0 = h * HEAD_DIM
        gate = _dot_nt(q_ref[0, :, c0:c0 + HEAD_DIM], km_ref[0, :, h, :],
                       precision=lax.Precision.HIGHEST)
        out = jnp.zeros((tn, LANES), jnp.int32)
        for r in range(n_top):
            best = jnp.max(gate, axis=1, keepdims=True)
            idx = jnp.min(jnp.where(gate == best, col, float(n_full)), axis=1, keepdims=True)
            out = jnp.where(lane == r, idx.astype(jnp.int32), out)
            gate = jnp.where(col == idx, NEG_INF, gate)
        o_ref[0, h] = out


def _decode_topk(q, kmeans, *, layer, n_top):
    db, tn, width = q.shape
    _, _, n_full, heads, hd = kmeans.shape
    return pl.pallas_call(
        functools.partial(_decode_topk_body, heads=heads, n_top=n_top),
        out_shape=jax.ShapeDtypeStruct((db, heads, tn, LANES), jnp.int32),
        grid=(db,),
        in_specs=[pl.BlockSpec((1, tn, width), lambda b: (b, 0, 0)),
                  pl.BlockSpec((None, 1, n_full, heads, hd), lambda b: (layer, b, 0, 0, 0))],
        out_specs=pl.BlockSpec((1, heads, tn, LANES), lambda b: (b, 0, 0, 0)),
        compiler_params=_compiler_params(("parallel",), 16 * 1024 * 1024),
        name="decode_topk",
    )(q, kmeans)


def _moba_decode_body(top_ref, pt_ref, bias_ref, q_ref, k_ref, v_ref, ck_ref, cv_ref, o_ref,
                      kbuf, vbuf, sems, *, layer, heads, tn, n_top, past):
    step = pl.program_id(0)
    n_steps = pl.num_programs(0)
    ppb = MOBA_BLOCK // PAGE_SIZE
    n_sel = tn * n_top * MOBA_BLOCK
    own_rows = LANES

    def gather(s, slot, start):
        b = s // heads
        h = s % heads
        for t in range(tn):
            for r in range(n_top):
                blk = top_ref[((b * heads + h) * tn + t) * n_top + r]
                for p in range(ppb):
                    phys = pt_ref[b, blk * ppb + p]
                    row0 = (t * n_top + r) * MOBA_BLOCK + p * PAGE_SIZE
                    for src, dst, sem in ((ck_ref, kbuf, sems.at[slot, 0]),
                                          (cv_ref, vbuf, sems.at[slot, 1])):
                        cp = pltpu.make_async_copy(
                            src.at[layer, phys, :, h, :],
                            dst.at[slot, pl.ds(row0, PAGE_SIZE), :], sem)
                        if start:
                            cp.start()
                        else:
                            cp.wait()

    slot = step % 2

    @pl.when(step == 0)
    def _():
        gather(step, 0, True)

    @pl.when(step + 1 < n_steps)
    def _():
        gather(step + 1, 1 - slot, True)

    gather(step, slot, False)

    b = step // heads
    h = step % heads
    scale = HEAD_DIM ** -0.5
    tp = q_ref.shape[1]
    qb = q_ref[0].astype(jnp.bfloat16)
    zeros = jnp.zeros((own_rows - tp, HEAD_DIM), jnp.float32)
    k_all = jnp.concatenate([kbuf[slot], k_ref[0], zeros], axis=0).astype(jnp.bfloat16)
    v_all = jnp.concatenate([vbuf[slot], v_ref[0], zeros], axis=0).astype(jnp.bfloat16)
    n_cols = n_sel + own_rows
    s = _dot_nt(qb, k_all) * scale

    row = lax.broadcasted_iota(jnp.int32, (tp, n_cols), 0)
    col = lax.broadcasted_iota(jnp.int32, (tp, n_cols), 1)
    seg = col // MOBA_BLOCK
    key_pos = past + (col - n_sel)
    for t in range(tn):
        for r in range(n_top):
            blk = top_ref[((b * heads + h) * tn + t) * n_top + r]
            key_pos = jnp.where(seg == t * n_top + r,
                                blk * MOBA_BLOCK + col % MOBA_BLOCK, key_pos)
    q_pos = past + row
    own = col >= n_sel
    valid = (own & ((col - n_sel) <= row)) | (jnp.logical_not(own) & (seg // n_top == row))
    bias = _t5_bias(jnp.maximum(q_pos - key_pos, 0), bias_ref, h)
    s = jnp.where(valid, s + bias, NEG_INF)
    m = jnp.max(s, axis=1, keepdims=True)
    p = jnp.exp(s - m)
    l = jnp.sum(p, axis=1, keepdims=True)
    o = jnp.dot(p.astype(jnp.bfloat16), v_all, preferred_element_type=jnp.float32)
    o_ref[0] = (o / l).astype(o_ref.dtype)


def _moba_decode(q, k, v, cache_k, cache_v, page_table, top_idx, bias_t, *, layer, heads):
    db, tp, width = q.shape
    tn, n_top = top_idx.shape[-2:]
    past = page_table.shape[1] * PAGE_SIZE
    n_sel = tn * n_top * MOBA_BLOCK
    spec = pl.BlockSpec((1, tp, HEAD_DIM), lambda s, *_: (s // heads, 0, s % heads))
    any_spec = pl.BlockSpec(memory_space=pl.ANY)
    vmem = 2 * 2 * n_sel * HEAD_DIM * 4 + 6 * (n_sel + LANES) * HEAD_DIM * 4
    return pl.pallas_call(
        functools.partial(_moba_decode_body, layer=layer, heads=heads, tn=tn, n_top=n_top,
                          past=past),
        out_shape=jax.ShapeDtypeStruct((db, tp, width), jnp.bfloat16),
        grid_spec=pltpu.PrefetchScalarGridSpec(
            num_scalar_prefetch=3,
            grid=(db * heads,),
            in_specs=[spec, spec, spec, any_spec, any_spec],
            out_specs=spec,
            scratch_shapes=[pltpu.VMEM((2, n_sel, HEAD_DIM), jnp.float32),
                            pltpu.VMEM((2, n_sel, HEAD_DIM), jnp.float32),
                            pltpu.SemaphoreType.DMA((2, 2))],
        ),
        compiler_params=_compiler_params(("arbitrary",), vmem + 8 * 1024 * 1024),
        name="moba_decode",
    )(top_idx.reshape(-1), page_table, bias_t, q, k, v, cache_k, cache_v)


def _hgrn2_gates(qb, fb, layer, lbs):
    q = _silu(qb)
    e = jnp.exp(-jnp.abs(fb))
    t = 1.0 / (1.0 + e)
    et = e * t
    pos = fb >= 0.0
    sig_neg = jnp.where(pos, et, t)
    if layer == 0:
        logf = jnp.minimum(fb, 0.0) - jnp.log(1.0 + e)
        k = sig_neg
    else:
        w = jnp.exp(lbs - jnp.max(lbs, axis=0, keepdims=True))
        p = w / jnp.sum(w, axis=0, keepdims=True)
        lb = jnp.sum(p[:layer + 1], axis=0, keepdims=True) - p[0:1]
        logf = jnp.log(lb + (1.0 - lb) * jnp.where(pos, t, et))
        k = (1.0 - lb) * sig_neg
    return q, logf, k


def _gla_level_ids(c, group):
    t = lax.broadcasted_iota(jnp.int32, (c, c), 0)
    s = lax.broadcasted_iota(jnp.int32, (c, c), 1)
    ids = jnp.full((c, c), -1, jnp.int32)
    size, j = 2 * group, 0
    while size <= c:
        half = size // 2
        hit = (t // size == s // size) & (t % size >= half) & (s % size < half)
        ids = jnp.where(hit, j, ids)
        size, j = 2 * size, j + 1
    return ids


def _gla_chunk(q, k, v, g, st, *, group, tril=None, level_ids=None):
    c, kd = q.shape
    g = g * LOG2_E
    if tril is None:
        row = lax.broadcasted_iota(jnp.int32, q.shape, 0)
        b = g
        sh = 1
        while sh < c:
            b = b + jnp.where(row >= sh, pltpu.roll(b, sh, 0), 0.0)
            sh *= 2
    else:
        hi = g.astype(jnp.bfloat16)
        r1 = g - hi.astype(jnp.float32)
        mid = r1.astype(jnp.bfloat16)
        lo = (r1 - mid.astype(jnp.float32)).astype(jnp.bfloat16)
        b = (jnp.dot(tril, hi, preferred_element_type=jnp.float32)
             + jnp.dot(tril, mid, preferred_element_type=jnp.float32)
             + jnp.dot(tril, lo, preferred_element_type=jnp.float32))

    shape3 = (c // SUBLANES, SUBLANES, kd)
    q3, k3, v3, b3 = (a.reshape(shape3) for a in (q, k, v, b))
    sub = lax.broadcasted_iota(jnp.int32, (1, SUBLANES, kd), 1) % group
    o3 = jnp.sum(q3 * k3, axis=2, keepdims=True) * v3
    for d in range(1, group):
        decay = jnp.exp2(jnp.where(sub >= d, b3 - pltpu.roll(b3, d, 1), NEG_INF))
        w = jnp.sum(q3 * decay * pltpu.roll(k3, d, 1), axis=2, keepdims=True)
        o3 = o3 + w * pltpu.roll(v3, d, 1)
    o = o3.reshape(c, v.shape[1])

    if c > group:
        row = lax.broadcasted_iota(jnp.int32, q.shape, 0)
        scores = jnp.zeros((c, c), jnp.float32)
        size, j = 2 * group, 0
        while size <= c:
            half = size // 2
            bs = b.reshape(c // size, size, kd)
            ref = jnp.broadcast_to(bs[:, half - 1:half, :], bs.shape).reshape(b.shape)
            upper = (row % size) >= half
            diff = b - ref
            x = (jnp.where(upper, q, k) * jnp.exp2(jnp.where(upper, diff, -diff))
                 ).astype(jnp.bfloat16)
            scores = jnp.where(level_ids == j, _dot_nt(x, x), scores)
            size, j = 2 * size, j + 1
        o = o + jnp.dot(scores.astype(jnp.bfloat16), v.astype(jnp.bfloat16),
                        preferred_element_type=jnp.float32)

    b_last = b[c - 1:c, :]
    o = o + _dot_nt((q * jnp.exp2(b)).astype(jnp.bfloat16), st.astype(jnp.bfloat16))
    kw = (k * jnp.exp2(b_last - b)).astype(jnp.bfloat16)
    st_new = st * jnp.exp2(b_last) + lax.dot_general(
        v.astype(jnp.bfloat16), kw, (((0,), (0,)), ((), ())), preferred_element_type=jnp.float32)
    return o, st_new


def _silu(x):
    return x / (1.0 + jnp.exp(-x))


def _hgrn2_out(o, gate, gain):
    ms = jnp.mean(o * o, axis=-1, keepdims=True)
    return o * lax.rsqrt(ms + EPS) * gain * _silu(gate)


def _hgrn2_prompt_body(qb_ref, fb_ref, ib_ref, gb_ref, lb_ref, gain_ref, o_ref, s_ref, st_ref,
                       *, layer, n_chunks):
    st_ref[...] = jnp.zeros_like(st_ref)
    lbs = lb_ref[...]
    gain = gain_ref[layer:layer + 1, :]
    tril = (lax.broadcasted_iota(jnp.int32, (GLA_CHUNK, GLA_CHUNK), 0)
            >= lax.broadcasted_iota(jnp.int32, (GLA_CHUNK, GLA_CHUNK), 1)).astype(jnp.bfloat16)
    level_ids = _gla_level_ids(GLA_CHUNK, GLA_GROUP)

    def chunk(i, carry):
        rows = pl.ds(pl.multiple_of(i * GLA_CHUNK, GLA_CHUNK), GLA_CHUNK)
        q, logf, k = _hgrn2_gates(qb_ref[rows, :], fb_ref[rows, :], layer, lbs)
        o, st = _gla_chunk(q, k, ib_ref[rows, :], logf, st_ref[...], group=GLA_GROUP, tril=tril,
                           level_ids=level_ids)
        st_ref[...] = st
        o_ref[rows, :] = _hgrn2_out(o, gb_ref[rows, :], gain).astype(o_ref.dtype)
        return carry

    lax.fori_loop(0, n_chunks, chunk, 0)
    s_ref[0, 0] = st_ref[...].T


def _hgrn2_prompt(proj, lower_logits, gain, *, layer, batch, seq, heads):
    m = proj.shape[0]
    assert seq % GLA_CHUNK == 0 and m == batch * seq
    depth = lower_logits.shape[0]

    def col(j):
        return pl.BlockSpec((seq, HG_K), lambda b, h: (b, j * heads + h))

    par = pl.BlockSpec((depth, HG_K), lambda b, h: (0, h))
    gspec = pl.BlockSpec((depth, HG_V), lambda b, h: (0, 0))
    vmem = 2 * (4 * seq * HG_K * 4 + seq * HG_V * 2) + 16 * GLA_CHUNK * GLA_CHUNK * 4
    return pl.pallas_call(
        functools.partial(_hgrn2_prompt_body, layer=layer, n_chunks=seq // GLA_CHUNK),
        out_shape=(jax.ShapeDtypeStruct((m, heads * HG_V), jnp.bfloat16),
                   jax.ShapeDtypeStruct((batch, heads, HG_K, HG_V), jnp.float32)),
        grid=(batch, heads),
        in_specs=[col(0), col(1), col(2), col(3), par, gspec],
        out_specs=(pl.BlockSpec((seq, HG_V), lambda b, h: (b, h)),
                   pl.BlockSpec((1, 1, HG_K, HG_V), lambda b, h: (b, h, 0, 0))),
        scratch_shapes=[pltpu.VMEM((HG_V, HG_K), jnp.float32)],
        compiler_params=_compiler_params(("parallel", "parallel"), vmem + 8 * 1024 * 1024),
        name="hgrn2_prompt",
    )(proj, proj, proj, proj, lower_logits, gain)


def _hgrn2_decode_body(qb_ref, fb_ref, ib_ref, gb_ref, s0_ref, lb_ref, gain_ref, o_ref, s_ref,
                       *, layer, tn):
    keep = lax.broadcasted_iota(jnp.int32, (SUBLANES, HG_K), 0) < tn
    q, logf, k = _hgrn2_gates(qb_ref[0], fb_ref[0], layer, lb_ref[...])
    logf = jnp.where(keep, logf, 0.0)
    k = jnp.where(keep, k, 0.0)
    o, st = _gla_chunk(q, k, ib_ref[0], logf, s0_ref[0, 0].T, group=SUBLANES)
    s_ref[0, 0] = st.T
    o_ref[0] = _hgrn2_out(o, gb_ref[0], gain_ref[layer:layer + 1, :]).astype(o_ref.dtype)


def _hgrn2_decode(proj, s0, lower_logits, gain, *, layer, heads, tn):
    db, tp, _ = proj.shape
    assert tn <= tp == SUBLANES
    depth = lower_logits.shape[0]

    def col(j):
        return pl.BlockSpec((1, tp, HG_K), lambda b, h: (b, 0, j * heads + h))

    sspec = pl.BlockSpec((1, 1, HG_K, HG_V), lambda b, h: (b, h, 0, 0))
    return pl.pallas_call(
        functools.partial(_hgrn2_decode_body, layer=layer, tn=tn),
        out_shape=(jax.ShapeDtypeStruct((db, tp, heads * HG_V), jnp.bfloat16),
                   jax.ShapeDtypeStruct(s0.shape, jnp.float32)),
        grid=(db, heads),
        in_specs=[col(0), col(1), col(2), col(3), sspec,
                  pl.BlockSpec((depth, HG_K), lambda b, h: (0, h)),
                  pl.BlockSpec((depth, HG_V), lambda b, h: (0, 0))],
        out_specs=(pl.BlockSpec((1, tp, HG_V), lambda b, h: (b, 0, h)), sspec),
        compiler_params=_compiler_params(("parallel", "parallel"), 16 * 1024 * 1024),
        name="hgrn2_decode",
    )(proj, proj, proj, proj, s0, lower_logits, gain)


def _trunk_layer(x, layer, weights, moba, hgrn2, *, a_width, b_width):
    w_in, w_ba, w_bb, w_out, w_up, w_down, g_mix, g_mlp = weights
    d = x[0].shape[1]
    f32, bf16 = jnp.float32, jnp.bfloat16
    h = [_rmsnorm(xi, g_mix[layer], bf16) for xi in x]
    proj = functools.partial(_matmul, h[0], h[1], w_in, layer=layer)
    qa = proj(col0=0, ncols=a_width, out_dtype=f32, name="proj_q")
    ka = proj(col0=a_width, ncols=a_width, out_dtype=f32, name="proj_k")
    va = proj(col0=2 * a_width, ncols=a_width, out_dtype=f32, name="proj_v")
    hg = proj(col0=3 * a_width, ncols=4 * b_width, out_dtype=f32, name="proj_hgrn")
    gates = proj(col0=3 * a_width + 4 * b_width, ncols=2 * d, out_dtype=bf16, act=_act_sigmoid,
                 name="proj_gates")
    o_a = [f(q, k, v) for f, q, k, v in zip(moba, qa, ka, va)]
    o_b, s_new = zip(*[f(g) for f, g in zip(hgrn2, hg)])
    merged = _merge_branches(o_a, o_b, w_ba, w_bb, gates, layer=layer)
    x = _matmul(*merged, w_out, layer=layer, out_dtype=f32, residual=x, name="out_proj")
    h2 = [_rmsnorm(xi, g_mlp[layer], bf16) for xi in x]
    u = _matmul(*h2, w_up, layer=layer, out_dtype=bf16, act=_act_relu2, name="ffn_up")
    x = _matmul(*u, w_down, layer=layer, out_dtype=f32, residual=x, name="ffn_down")
    return x, ka, va, s_new


def kernel(x_prompt, x_sample, cache_k, cache_v, state_hgrn, page_table, w_in, w_branch_a,
           w_branch_b, w_out, w_up, w_down, norm_mix, norm_mlp, hg_norm, hg_lower_bounds,
           rel_bias, final_norm):
    batch, seq, d = x_prompt.shape
    db, tn, _ = x_sample.shape
    depth = w_in.shape[0]
    a_width = w_branch_a.shape[1]
    b_width = w_branch_b.shape[1]
    a_heads = a_width // HEAD_DIM
    b_heads = b_width // HG_V
    n_pages = page_table.shape[1]
    n_full = (n_pages * PAGE_SIZE) // MOBA_BLOCK
    assert (n_pages * PAGE_SIZE) % MOBA_BLOCK == 0
    n_top = min(MOBA_TOPK, n_full)
    assert n_top > 0 and tn <= SUBLANES

    bf16 = jnp.bfloat16
    bias_t = rel_bias.astype(jnp.float32).T
    tiles = _bias_tiles(bias_t)
    kmeans = _page_block_means(cache_k, page_table)

    x = (x_prompt.reshape(batch * seq, d), x_sample.reshape(db * tn, d))
    weights = (w_in.astype(bf16), w_branch_a.astype(bf16), w_branch_b.astype(bf16),
               w_out.astype(bf16), w_up.astype(bf16), w_down.astype(bf16), norm_mix, norm_mlp)
    outs = {name: [] for name in ("kp", "vp", "sp", "ks", "vs", "ss")}
    for l in range(depth):

        def moba_p(q, k, v):
            return _moba_prompt(q, k, v, bias_t, tiles, batch=batch, seq=seq)

        def hgrn_p(hg, l=l):
            return _hgrn2_prompt(hg, hg_lower_bounds, hg_norm, layer=l, batch=batch, seq=seq,
                                 heads=b_heads)

        def pad_tokens(a):
            return jnp.pad(a.reshape(db, tn, -1), ((0, 0), (0, SUBLANES - tn), (0, 0)))

        def moba_s(q, k, v, l=l):
            q3, k3, v3 = pad_tokens(q), pad_tokens(k), pad_tokens(v)
            top = _decode_topk(q3, kmeans, layer=l, n_top=n_top)[:, :, :tn, :n_top]
            o = _moba_decode(q3, k3, v3, cache_k, cache_v, page_table, top, bias_t, layer=l,
                             heads=a_heads)
            return o[:, :tn].reshape(db * tn, a_width)

        def hgrn_s(hg, l=l):
            o, s = _hgrn2_decode(pad_tokens(hg), state_hgrn[l], hg_lower_bounds, hg_norm,
                                 layer=l, heads=b_heads, tn=tn)
            return o[:, :tn].reshape(db * tn, b_width), s

        x, (kp, ks), (vp, vs), (sp, ss) = _trunk_layer(
            x, l, weights, (moba_p, moba_s), (hgrn_p, hgrn_s), a_width=a_width, b_width=b_width)
        outs["kp"].append(kp.reshape(batch, seq, a_heads, HEAD_DIM))
        outs["vp"].append(vp.reshape(batch, seq, a_heads, HEAD_DIM))
        outs["sp"].append(sp)
        outs["ks"].append(ks.reshape(db, tn, a_heads, HEAD_DIM))
        outs["vs"].append(vs.reshape(db, tn, a_heads, HEAD_DIM))
        outs["ss"].append(ss)

    y_prompt = _rmsnorm(x[0], final_norm, jnp.float32).reshape(batch, seq, d)
    y_sample = _rmsnorm(x[1], final_norm, jnp.float32).reshape(db, tn, d)
    return (y_prompt, y_sample, jnp.stack(outs["kp"]), jnp.stack(outs["vp"]),
            jnp.stack(outs["sp"]), jnp.stack(outs["ks"]), jnp.stack(outs["vs"]),
            jnp.stack(outs["ss"]))
```

```python
import functools
import math

import jax
import jax.numpy as jnp
from jax import lax
from jax.experimental import pallas as pl
from jax.experimental.pallas import tpu as pltpu

HEAD_DIM = 128
MOBA_BLOCK = 256
MOBA_TOPK = 3
NUM_BUCKETS = 32
MAX_DISTANCE = 128
PAGE_SIZE = 128
HG_K = 128
HG_V = 128
EPS = 1e-6

V7X_VMEM_LIMIT_BYTES = 60000 * 1024
SUBLANES = 8
LANES = 128

GLA_CHUNK = 256
GLA_GROUP = 4
LOG2_E = math.log2(math.e)
NEG_INF = float("-inf")


def _compiler_params(semantics, vmem_bytes):
    limit = int(min(max(vmem_bytes, 16 * 1024 * 1024), V7X_VMEM_LIMIT_BYTES))
    return pltpu.CompilerParams(dimension_semantics=semantics, vmem_limit_bytes=limit)


def _rmsnorm_body(x_ref, g_ref, o_ref):
    x = x_ref[...]
    ms = jnp.mean(x * x, axis=-1, keepdims=True)
    o_ref[...] = (x * lax.rsqrt(ms + EPS) * g_ref[...]).astype(o_ref.dtype)


def _rmsnorm(x, gain, out_dtype):
    m, d = x.shape
    tm = min(m, 256)
    assert m % tm == 0
    block_bytes = tm * d * (4 + jnp.dtype(out_dtype).itemsize)
    return pl.pallas_call(
        _rmsnorm_body,
        out_shape=jax.ShapeDtypeStruct((m, d), out_dtype),
        grid=(m // tm,),
        in_specs=[pl.BlockSpec((tm, d), lambda i: (i, 0)),
                  pl.BlockSpec((1, d), lambda i: (0, 0))],
        out_specs=pl.BlockSpec((tm, d), lambda i: (i, 0)),
        compiler_params=_compiler_params(("parallel",), 4 * block_bytes),
        name="rmsnorm",
    )(x, gain.reshape(1, d))


def _act_none(a):
    return a


def _act_sigmoid(a):
    return jax.nn.sigmoid(a)


def _act_relu2(a):
    return jnp.square(jnp.maximum(a, 0.0))


def _mm_body(xp_ref, xs_ref, w_ref, *refs, act, has_res, nk):
    op_ref, os_ref = refs[-2:]
    rp_ref, rs_ref = refs[:2] if has_res else (None, None)
    i = pl.program_id(1)
    k = pl.program_id(2)

    def emit(x_ref, r_ref, o_ref):
        if nk == 1:
            acc = jnp.dot(x_ref[...], w_ref[...], preferred_element_type=jnp.float32)
            if has_res:
                acc = acc + r_ref[...]
            o_ref[...] = act(acc).astype(o_ref.dtype)
        else:
            @pl.when(k == 0)
            def _():
                o_ref[...] = r_ref[...] if has_res else jnp.zeros_like(o_ref)

            o_ref[...] += jnp.dot(x_ref[...], w_ref[...], preferred_element_type=jnp.float32)

    emit(xp_ref, rp_ref, op_ref)

    @pl.when(i == 0)
    def _():
        emit(xs_ref, rs_ref, os_ref)


def _matmul(xp, xs, w, *, layer, col0=0, ncols=None, out_dtype, act=_act_none, residual=None,
            stack=None, name):
    m, kdim = xp.shape
    ms = xs.shape[0]
    ncols = w.shape[2] if ncols is None else ncols
    tm = min(m, 1024)
    tk = min(kdim, 4096)
    tn = math.gcd(ncols, col0, 1024)
    assert m % tm == 0 and kdim % tk == 0 and ncols % tn == 0 and col0 % tn == 0
    nm, nn, nk = m // tm, ncols // tn, kdim // tk
    assert nk == 1 or (out_dtype == jnp.float32 and act is _act_none)
    c0 = col0 // tn
    in_specs = [pl.BlockSpec((tm, tk), lambda n, i, k: (i, k)),
                pl.BlockSpec((ms, tk), lambda n, i, k: (0, k)),
                pl.BlockSpec((None, tk, tn), lambda n, i, k: (layer, k, n + c0))]
    args = [xp, xs, w]
    osize = jnp.dtype(out_dtype).itemsize
    vmem = (2 * ((tm + ms) * tk * 2 + tk * tn * 2 + (tm + ms) * tn * osize)
            + 2 * (tm + ms) * tn * 4)
    if residual is not None:
        in_specs += [pl.BlockSpec((tm, tn), lambda n, i, k: (i, n)),
                     pl.BlockSpec((ms, tn), lambda n, i, k: (0, n))]
        args += list(residual)
        vmem += 2 * (tm + ms) * tn * 4
    aliases = {}
    if stack is None:
        p_shape = jax.ShapeDtypeStruct((m, ncols), out_dtype)
        p_spec = pl.BlockSpec((tm, tn), lambda n, i, k: (i, n))
    else:
        assert stack.shape[1:] == (m, ncols) and stack.dtype == out_dtype
        p_shape = jax.ShapeDtypeStruct(stack.shape, out_dtype)
        p_spec = pl.BlockSpec((None, tm, tn), lambda n, i, k: (layer, i, n))
        in_specs.append(pl.BlockSpec(memory_space=pl.ANY))
        args.append(stack)
        aliases = {len(args) - 1: 0}
    return pl.pallas_call(
        functools.partial(_mm_body, act=act, has_res=residual is not None, nk=nk),
        out_shape=(p_shape, jax.ShapeDtypeStruct((ms, ncols), out_dtype)),
        grid=(nn, nm, nk),
        in_specs=in_specs,
        out_specs=(p_spec, pl.BlockSpec((ms, tn), lambda n, i, k: (0, n))),
        input_output_aliases=aliases,
        compiler_params=_compiler_params(("arbitrary", "arbitrary", "arbitrary"),
                                         vmem + 4 * 1024 * 1024),
        name=name,
    )(*args)


def _merge_body(oap_ref, obp_ref, oas_ref, obs_ref, wa_ref, wb_ref, gap_ref, gbp_ref, gas_ref,
                gbs_ref, op_ref, os_ref):
    def emit(oa_ref, ob_ref, ga_ref, gb_ref, o_ref):
        a = jnp.dot(oa_ref[...], wa_ref[...], preferred_element_type=jnp.float32)
        b = jnp.dot(ob_ref[...], wb_ref[...], preferred_element_type=jnp.float32)
        o_ref[...] = (ga_ref[...].astype(jnp.float32) * a
                      + gb_ref[...].astype(jnp.float32) * b).astype(o_ref.dtype)

    emit(oap_ref, obp_ref, gap_ref, gbp_ref, op_ref)

    @pl.when(pl.program_id(1) == 0)
    def _():
        emit(oas_ref, obs_ref, gas_ref, gbs_ref, os_ref)


def _merge_branches(o_a, o_b, w_ba, w_bb, gates, *, layer):
    (oap, oas), (obp, obs), (gp, gs) = o_a, o_b, gates
    m, ka = oap.shape
    ms = oas.shape[0]
    kb = obp.shape[1]
    d = w_ba.shape[2]
    tm = min(m, 1024)
    tn = min(d, 512)
    assert m % tm == 0 and d % tn == 0
    goff = d // tn
    rows = tm + ms
    vmem = 2 * 2 * (rows * ka + rows * kb + ka * tn + kb * tn + 3 * rows * tn) + 3 * rows * tn * 4
    return pl.pallas_call(
        _merge_body,
        out_shape=(jax.ShapeDtypeStruct((m, d), jnp.bfloat16),
                   jax.ShapeDtypeStruct((ms, d), jnp.bfloat16)),
        grid=(d // tn, m // tm),
        in_specs=[pl.BlockSpec((tm, ka), lambda n, i: (i, 0)),
                  pl.BlockSpec((tm, kb), lambda n, i: (i, 0)),
                  pl.BlockSpec((ms, ka), lambda n, i: (0, 0)),
                  pl.BlockSpec((ms, kb), lambda n, i: (0, 0)),
                  pl.BlockSpec((None, ka, tn), lambda n, i: (layer, 0, n)),
                  pl.BlockSpec((None, kb, tn), lambda n, i: (layer, 0, n)),
                  pl.BlockSpec((tm, tn), lambda n, i: (i, n)),
                  pl.BlockSpec((tm, tn), lambda n, i: (i, n + goff)),
                  pl.BlockSpec((ms, tn), lambda n, i: (0, n)),
                  pl.BlockSpec((ms, tn), lambda n, i: (0, n + goff))],
        out_specs=(pl.BlockSpec((tm, tn), lambda n, i: (i, n)),
                   pl.BlockSpec((ms, tn), lambda n, i: (0, n))),
        compiler_params=_compiler_params(("arbitrary", "arbitrary"), vmem + 4 * 1024 * 1024),
        name="merge_branches",
    )(oap, obp, oas, obs, w_ba, w_bb, gp, gp, gs, gs)


def _t5_bias(dist, bias_ref, head):
    max_exact = NUM_BUCKETS // 2
    nf = jnp.maximum(dist, 1).astype(jnp.float32)
    far = max_exact + (jnp.log(nf / max_exact) / math.log(MAX_DISTANCE / max_exact)
                       * (NUM_BUCKETS - max_exact)).astype(jnp.int32)
    far = jnp.minimum(far, NUM_BUCKETS - 1)
    bucket = jnp.where(dist < max_exact, dist, far)
    out = jnp.zeros(dist.shape, jnp.float32)
    for k in range(NUM_BUCKETS):
        out = jnp.where(bucket == k, bias_ref[head, k], out)
    return out


def _bias_tiles_body(bias_ref, o_ref):
    h = pl.program_id(0)
    row = lax.broadcasted_iota(jnp.int32, (MOBA_BLOCK, MOBA_BLOCK), 0)
    col = lax.broadcasted_iota(jnp.int32, (MOBA_BLOCK, MOBA_BLOCK), 1)
    own = _t5_bias(jnp.maximum(row - col, 0), bias_ref, h)
    o_ref[0, 0] = jnp.where(row >= col, own, NEG_INF)
    o_ref[0, 1] = _t5_bias(row - col + MOBA_BLOCK, bias_ref, h)


def _bias_tiles(bias_t):
    h = bias_t.shape[0]
    return pl.pallas_call(
        _bias_tiles_body,
        out_shape=jax.ShapeDtypeStruct((h, 2, MOBA_BLOCK, MOBA_BLOCK), jnp.float32),
        grid_spec=pltpu.PrefetchScalarGridSpec(
            num_scalar_prefetch=1,
            grid=(h,),
            in_specs=[],
            out_specs=pl.BlockSpec((1, 2, MOBA_BLOCK, MOBA_BLOCK), lambda i, b: (i, 0, 0, 0)),
        ),
        compiler_params=_compiler_params(("arbitrary",), 8 * 1024 * 1024),
        name="t5_bias_tiles",
    )(bias_t)


def _dot_nt(a, b, precision=None):
    return lax.dot_general(a, b, (((1,), (1,)), ((), ())), precision=precision,
                           preferred_element_type=jnp.float32)


def _moba_prompt_body(bias_ref, q_ref, k_ref, v_ref, tiles_ref, o_ref, *, n_blk):
    h = pl.program_id(0)
    t = n_blk * MOBA_BLOCK
    n_top = min(MOBA_TOPK, n_blk - 1)
    scale = HEAD_DIM ** -0.5
    q = q_ref[...]
    k = k_ref[...]
    qb = (q * scale).astype(jnp.bfloat16)
    kb = k.astype(jnp.bfloat16)
    vb = v_ref[...].astype(jnp.bfloat16)
    own_tile = tiles_ref[0, 0]
    adj_tile = tiles_ref[0, 1]

    if n_top > 0:
        kmean = jnp.mean(k.reshape(n_blk, MOBA_BLOCK, HEAD_DIM), axis=1)
        gate = _dot_nt(kmean, q, precision=lax.Precision.HIGHEST)
        nidx = lax.broadcasted_iota(jnp.int32, (n_blk, t), 0)
        qblk = lax.broadcasted_iota(jnp.int32, (n_blk, t), 1) // MOBA_BLOCK
        rank = jnp.zeros((n_blk, t), jnp.int32)
        for n2 in range(n_blk):
            g2 = gate[n2:n2 + 1, :]
            beats = (n2 < qblk) & ((g2 > gate) | ((g2 == gate) & (n2 < nidx)))
            rank = rank + beats.astype(jnp.int32)
        sel_t = ((nidx < qblk) & (rank < n_top)).astype(jnp.float32)
        pad = jnp.zeros((LANES - n_blk, t), jnp.float32)
        sel = jnp.concatenate([sel_t, pad], axis=0).T
        far_bias = bias_ref[h, NUM_BUCKETS - 1]

    for j in range(n_blk):
        r0 = j * MOBA_BLOCK
        nk = r0 + MOBA_BLOCK
        if j == 0 or n_top == 0:
            s = _dot_nt(qb[r0:nk], kb[r0:nk]) + own_tile
            vals = vb[r0:nk]
        else:
            s = _dot_nt(qb[r0:nk], kb[:nk])
            parts = []
            for n in range(j):
                near = n == j - 1
                gate_col = jnp.where(sel[r0:nk, n:n + 1] > 0.5, 0.0 if near else far_bias, NEG_INF)
                tile = s[:, n * MOBA_BLOCK:(n + 1) * MOBA_BLOCK] + gate_col
                parts.append(tile + adj_tile if near else tile)
            parts.append(s[:, r0:nk] + own_tile)
            s = jnp.concatenate(parts, axis=1)
            vals = vb[:nk]
        m = jnp.max(s, axis=1, keepdims=True)
        p = jnp.exp(s - m)
        l = jnp.sum(p, axis=1, keepdims=True)
        o = jnp.dot(p.astype(jnp.bfloat16), vals, preferred_element_type=jnp.float32)
        o_ref[r0:nk, :] = (o / l).astype(o_ref.dtype)


def _moba_prompt(q, k, v, bias_t, tiles, *, layer, batch, seq):
    m, width = q.shape
    heads = width // HEAD_DIM
    assert seq % MOBA_BLOCK == 0 and m == batch * seq
    n_blk = seq // MOBA_BLOCK
    spec = pl.BlockSpec((seq, HEAD_DIM), lambda h, b, bias: (b, h))
    kv_spec = pl.BlockSpec((None, seq, HEAD_DIM), lambda h, b, bias: (layer, b, h))
    vmem = 2 * (3 * seq * HEAD_DIM * 4 + seq * HEAD_DIM * 2) + 8 * MOBA_BLOCK * seq * 4
    return pl.pallas_call(
        functools.partial(_moba_prompt_body, n_blk=n_blk),
        out_shape=jax.ShapeDtypeStruct((m, width), jnp.bfloat16),
        grid_spec=pltpu.PrefetchScalarGridSpec(
            num_scalar_prefetch=1,
            grid=(heads, batch),
            in_specs=[spec, kv_spec, kv_spec,
                      pl.BlockSpec((1, 2, MOBA_BLOCK, MOBA_BLOCK),
                                   lambda h, b, bias: (h, 0, 0, 0))],
            out_specs=spec,
        ),
        compiler_params=_compiler_params(("parallel", "parallel"), vmem + 8 * 1024 * 1024),
        name="moba_prompt",
    )(bias_t, q, k, v, tiles)


PAGES_PER_STEP = 8


def _page_mean_body(pt_ref, *refs, ppb):
    o_ref = refs[-1]
    i = pl.program_id(2)
    blocks = (len(refs) - 1) // ppb
    for j in range(blocks):
        total = sum(jnp.sum(refs[j * ppb + p][0, 0], axis=0) for p in range(ppb))
        o_ref[0, 0, i * blocks + j] = total * (1.0 / MOBA_BLOCK)


def _page_block_means(cache_k, page_table):
    depth, _, page, heads, hd = cache_k.shape
    db, n_pages = page_table.shape
    ppb = MOBA_BLOCK // PAGE_SIZE
    assert page == PAGE_SIZE and n_pages % ppb == 0
    n_full = n_pages // ppb
    pages = math.gcd(n_pages, PAGES_PER_STEP)
    assert pages % ppb == 0

    def page_spec(j):
        return pl.BlockSpec((1, 1, page, heads, hd),
                            lambda l, b, i, pt: (l, pt[b, i * pages + j], 0, 0, 0))

    return pl.pallas_call(
        functools.partial(_page_mean_body, ppb=ppb),
        out_shape=jax.ShapeDtypeStruct((depth, db, n_full, heads, hd), jnp.float32),
        grid_spec=pltpu.PrefetchScalarGridSpec(
            num_scalar_prefetch=1,
            grid=(depth, db, n_pages // pages),
            in_specs=[page_spec(j) for j in range(pages)],
            out_specs=pl.BlockSpec((1, 1, n_full, heads, hd), lambda l, b, i, pt: (l, b, 0, 0, 0)),
        ),
        compiler_params=_compiler_params(("arbitrary",) * 3,
                                         (2 * pages + 4) * page * heads * hd * 4),
        name="page_block_means",
    )(page_table, *([cache_k] * pages))


def _decode_topk_body(q_ref, km_ref, o_ref, *, heads, n_top):
    tn = q_ref.shape[1]
    n_full = km_ref.shape[1]
    lane = lax.broadcasted_iota(jnp.int32, (tn, LANES), 1)
    col = lax.broadcasted_iota(jnp.int32, (tn, n_full), 1).astype(jnp.float32)
    for h in range(heads):
        c0 = h * HEAD_DIM
        gate = _dot_nt(q_ref[0, :, c0:c0 + HEAD_DIM], km_ref[0, :, h, :],
                       precision=lax.Precision.HIGHEST)
        out = jnp.zeros((tn, LANES), jnp.int32)
        for r in range(n_top):
            best = jnp.max(gate, axis=1, keepdims=True)
            idx = jnp.min(jnp.where(gate == best, col, float(n_full)), axis=1, keepdims=True)
            out = jnp.where(lane == r, idx.astype(jnp.int32), out)
            gate = jnp.where(col == idx, NEG_INF, gate)
        o_ref[0, h] = out


def _decode_topk(q, kmeans, *, layer, n_top):
    db, tn, width = q.shape
    _, _, n_full, heads, hd = kmeans.shape
    return pl.pallas_call(
        functools.partial(_decode_topk_body, heads=heads, n_top=n_top),
        out_shape=jax.ShapeDtypeStruct((db, heads, tn, LANES), jnp.int32),
        grid=(db,),
        in_specs=[pl.BlockSpec((1, tn, width), lambda b: (b, 0, 0)),
                  pl.BlockSpec((None, 1, n_full, heads, hd), lambda b: (layer, b, 0, 0, 0))],
        out_specs=pl.BlockSpec((1, heads, tn, LANES), lambda b: (b, 0, 0, 0)),
        compiler_params=_compiler_params(("parallel",), 16 * 1024 * 1024),
        name="decode_topk",
    )(q, kmeans)


def _moba_decode_body(top_ref, pt_ref, bias_ref, q_ref, k_ref, v_ref, ck_ref, cv_ref, o_ref,
                      kbuf, vbuf, sems, *, layer, heads, tn, n_top, past):
    step = pl.program_id(0)
    n_steps = pl.num_programs(0)
    ppb = MOBA_BLOCK // PAGE_SIZE
    n_sel = tn * n_top * MOBA_BLOCK
    own_rows = LANES

    def gather(s, slot, start):
        b = s // heads
        h = s % heads
        for t in range(tn):
            for r in range(n_top):
                blk = top_ref[((b * heads + h) * tn + t) * n_top + r]
                for p in range(ppb):
                    phys = pt_ref[b, blk * ppb + p]
                    row0 = (t * n_top + r) * MOBA_BLOCK + p * PAGE_SIZE
                    for src, dst, sem in ((ck_ref, kbuf, sems.at[slot, 0]),
                                          (cv_ref, vbuf, sems.at[slot, 1])):
                        cp = pltpu.make_async_copy(
                            src.at[layer, phys, :, h, :],
                            dst.at[slot, pl.ds(row0, PAGE_SIZE), :], sem)
                        if start:
                            cp.start()
                        else:
                            cp.wait()

    slot = step % 2

    @pl.when(step == 0)
    def _():
        gather(step, 0, True)

    @pl.when(step + 1 < n_steps)
    def _():
        gather(step + 1, 1 - slot, True)

    gather(step, slot, False)

    b = step // heads
    h = step % heads
    scale = HEAD_DIM ** -0.5
    tp = q_ref.shape[1]
    qb = q_ref[0].astype(jnp.bfloat16)
    zeros = jnp.zeros((own_rows - tp, HEAD_DIM), jnp.float32)
    k_all = jnp.concatenate([kbuf[slot], k_ref[0], zeros], axis=0).astype(jnp.bfloat16)
    v_all = jnp.concatenate([vbuf[slot], v_ref[0], zeros], axis=0).astype(jnp.bfloat16)
    n_cols = n_sel + own_rows
    s = _dot_nt(qb, k_all) * scale

    row = lax.broadcasted_iota(jnp.int32, (tp, n_cols), 0)
    col = lax.broadcasted_iota(jnp.int32, (tp, n_cols), 1)
    seg = col // MOBA_BLOCK
    key_pos = past + (col - n_sel)
    for t in range(tn):
        for r in range(n_top):
            blk = top_ref[((b * heads + h) * tn + t) * n_top + r]
            key_pos = jnp.where(seg == t * n_top + r,
                                blk * MOBA_BLOCK + col % MOBA_BLOCK, key_pos)
    q_pos = past + row
    own = col >= n_sel
    valid = (own & ((col - n_sel) <= row)) | (jnp.logical_not(own) & (seg // n_top == row))
    bias = _t5_bias(jnp.maximum(q_pos - key_pos, 0), bias_ref, h)
    s = jnp.where(valid, s + bias, NEG_INF)
    m = jnp.max(s, axis=1, keepdims=True)
    p = jnp.exp(s - m)
    l = jnp.sum(p, axis=1, keepdims=True)
    o = jnp.dot(p.astype(jnp.bfloat16), v_all, preferred_element_type=jnp.float32)
    o_ref[0] = (o / l).astype(o_ref.dtype)


def _moba_decode(q, k, v, cache_k, cache_v, page_table, top_idx, bias_t, *, layer, heads):
    db, tp, width = q.shape
    tn, n_top = top_idx.shape[-2:]
    past = page_table.shape[1] * PAGE_SIZE
    n_sel = tn * n_top * MOBA_BLOCK
    spec = pl.BlockSpec((1, tp, HEAD_DIM), lambda s, *_: (s // heads, 0, s % heads))
    any_spec = pl.BlockSpec(memory_space=pl.ANY)
    vmem = 2 * 2 * n_sel * HEAD_DIM * 4 + 6 * (n_sel + LANES) * HEAD_DIM * 4
    return pl.pallas_call(
        functools.partial(_moba_decode_body, layer=layer, heads=heads, tn=tn, n_top=n_top,
                          past=past),
        out_shape=jax.ShapeDtypeStruct((db, tp, width), jnp.bfloat16),
        grid_spec=pltpu.PrefetchScalarGridSpec(
            num_scalar_prefetch=3,
            grid=(db * heads,),
            in_specs=[spec, spec, spec, any_spec, any_spec],
            out_specs=spec,
            scratch_shapes=[pltpu.VMEM((2, n_sel, HEAD_DIM), jnp.float32),
                            pltpu.VMEM((2, n_sel, HEAD_DIM), jnp.float32),
                            pltpu.SemaphoreType.DMA((2, 2))],
        ),
        compiler_params=_compiler_params(("arbitrary",), vmem + 8 * 1024 * 1024),
        name="moba_decode",
    )(top_idx.reshape(-1), page_table, bias_t, q, k, v, cache_k, cache_v)


def _hgrn2_gates(qb, fb, layer, lbs):
    q = _silu(qb)
    e = jnp.exp(-jnp.abs(fb))
    t = 1.0 / (1.0 + e)
    et = e * t
    pos = fb >= 0.0
    sig_neg = jnp.where(pos, et, t)
    if layer == 0:
        logf = jnp.minimum(fb, 0.0) - jnp.log(1.0 + e)
        k = sig_neg
    else:
        w = jnp.exp(lbs - jnp.max(lbs, axis=0, keepdims=True))
        p = w / jnp.sum(w, axis=0, keepdims=True)
        lb = jnp.sum(p[:layer + 1], axis=0, keepdims=True) - p[0:1]
        logf = jnp.log(lb + (1.0 - lb) * jnp.where(pos, t, et))
        k = (1.0 - lb) * sig_neg
    return q, logf, k


def _gla_level_ids(c, group):
    t = lax.broadcasted_iota(jnp.int32, (c, c), 0)
    s = lax.broadcasted_iota(jnp.int32, (c, c), 1)
    ids = jnp.full((c, c), -1, jnp.int32)
    size, j = 2 * group, 0
    while size <= c:
        half = size // 2
        hit = (t // size == s // size) & (t % size >= half) & (s % size < half)
        ids = jnp.where(hit, j, ids)
        size, j = 2 * size, j + 1
    return ids


def _gla_chunk(q, k, v, g, st, *, group, tril=None, level_ids=None):
    c, kd = q.shape
    g = g * LOG2_E
    if tril is None:
        row = lax.broadcasted_iota(jnp.int32, q.shape, 0)
        b = g
        sh = 1
        while sh < c:
            b = b + jnp.where(row >= sh, pltpu.roll(b, sh, 0), 0.0)
            sh *= 2
    else:
        hi = g.astype(jnp.bfloat16)
        r1 = g - hi.astype(jnp.float32)
        mid = r1.astype(jnp.bfloat16)
        lo = (r1 - mid.astype(jnp.float32)).astype(jnp.bfloat16)
        b = (jnp.dot(tril, hi, preferred_element_type=jnp.float32)
             + jnp.dot(tril, mid, preferred_element_type=jnp.float32)
             + jnp.dot(tril, lo, preferred_element_type=jnp.float32))

    shape3 = (c // SUBLANES, SUBLANES, kd)
    q3, k3, v3, b3 = (a.reshape(shape3) for a in (q, k, v, b))
    sub = lax.broadcasted_iota(jnp.int32, (1, SUBLANES, kd), 1) % group
    o3 = jnp.sum(q3 * k3, axis=2, keepdims=True) * v3
    for d in range(1, group):
        decay = jnp.exp2(jnp.where(sub >= d, b3 - pltpu.roll(b3, d, 1), NEG_INF))
        w = jnp.sum(q3 * decay * pltpu.roll(k3, d, 1), axis=2, keepdims=True)
        o3 = o3 + w * pltpu.roll(v3, d, 1)
    o = o3.reshape(c, v.shape[1])

    if c > group:
        row = lax.broadcasted_iota(jnp.int32, q.shape, 0)
        scores = jnp.zeros((c, c), jnp.float32)
        size, j = 2 * group, 0
        while size <= c:
            half = size // 2
            bs = b.reshape(c // size, size, kd)
            ref = jnp.broadcast_to(bs[:, half - 1:half, :], bs.shape).reshape(b.shape)
            upper = (row % size) >= half
            diff = b - ref
            x = (jnp.where(upper, q, k) * jnp.exp2(jnp.where(upper, diff, -diff))
                 ).astype(jnp.bfloat16)
            scores = jnp.where(level_ids == j, _dot_nt(x, x), scores)
            size, j = 2 * size, j + 1
        o = o + jnp.dot(scores.astype(jnp.bfloat16), v.astype(jnp.bfloat16),
                        preferred_element_type=jnp.float32)

    b_last = b[c - 1:c, :]
    o = o + _dot_nt((q * jnp.exp2(b)).astype(jnp.bfloat16), st.astype(jnp.bfloat16))
    kw = (k * jnp.exp2(b_last - b)).astype(jnp.bfloat16)
    st_new = st * jnp.exp2(b_last) + lax.dot_general(
        v.astype(jnp.bfloat16), kw, (((0,), (0,)), ((), ())), preferred_element_type=jnp.float32)
    return o, st_new


def _silu(x):
    return x / (1.0 + jnp.exp(-x))


def _hgrn2_out(o, gate, gain):
    ms = jnp.mean(o * o, axis=-1, keepdims=True)
    return o * lax.rsqrt(ms + EPS) * gain * _silu(gate)


def _hgrn2_prompt_body(qb_ref, fb_ref, ib_ref, gb_ref, lb_ref, gain_ref, o_ref, s_ref, st_ref,
                       *, layer, n_chunks):
    st_ref[...] = jnp.zeros_like(st_ref)
    lbs = lb_ref[...]
    gain = gain_ref[layer:layer + 1, :]
    tril = (lax.broadcasted_iota(jnp.int32, (GLA_CHUNK, GLA_CHUNK), 0)
            >= lax.broadcasted_iota(jnp.int32, (GLA_CHUNK, GLA_CHUNK), 1)).astype(jnp.bfloat16)
    level_ids = _gla_level_ids(GLA_CHUNK, GLA_GROUP)

    def chunk(i, carry):
        rows = pl.ds(pl.multiple_of(i * GLA_CHUNK, GLA_CHUNK), GLA_CHUNK)
        q, logf, k = _hgrn2_gates(qb_ref[rows, :], fb_ref[rows, :], layer, lbs)
        o, st = _gla_chunk(q, k, ib_ref[rows, :], logf, st_ref[...], group=GLA_GROUP, tril=tril,
                           level_ids=level_ids)
        st_ref[...] = st
        o_ref[rows, :] = _hgrn2_out(o, gb_ref[rows, :], gain).astype(o_ref.dtype)
        return carry

    lax.fori_loop(0, n_chunks, chunk, 0, unroll=2 if n_chunks % 2 == 0 else 1)
    s_ref[0, 0] = st_ref[...].T


def _hgrn2_prompt(proj, lower_logits, gain, *, layer, batch, seq, heads):
    m = proj.shape[0]
    assert seq % GLA_CHUNK == 0 and m == batch * seq
    depth = lower_logits.shape[0]

    def col(j):
        return pl.BlockSpec((seq, HG_K), lambda b, h: (b, j * heads + h))

    par = pl.BlockSpec((depth, HG_K), lambda b, h: (0, h))
    gspec = pl.BlockSpec((depth, HG_V), lambda b, h: (0, 0))
    vmem = 2 * (4 * seq * HG_K * 4 + seq * HG_V * 2) + 16 * GLA_CHUNK * GLA_CHUNK * 4
    return pl.pallas_call(
        functools.partial(_hgrn2_prompt_body, layer=layer, n_chunks=seq // GLA_CHUNK),
        out_shape=(jax.ShapeDtypeStruct((m, heads * HG_V), jnp.bfloat16),
                   jax.ShapeDtypeStruct((batch, heads, HG_K, HG_V), jnp.float32)),
        grid=(batch, heads),
        in_specs=[col(0), col(1), col(2), col(3), par, gspec],
        out_specs=(pl.BlockSpec((seq, HG_V), lambda b, h: (b, h)),
                   pl.BlockSpec((1, 1, HG_K, HG_V), lambda b, h: (b, h, 0, 0))),
        scratch_shapes=[pltpu.VMEM((HG_V, HG_K), jnp.float32)],
        compiler_params=_compiler_params(("parallel", "parallel"), vmem + 8 * 1024 * 1024),
        name="hgrn2_prompt",
    )(proj, proj, proj, proj, lower_logits, gain)


def _hgrn2_decode_body(qb_ref, fb_ref, ib_ref, gb_ref, s0_ref, lb_ref, gain_ref, o_ref, s_ref,
                       *, layer, tn):
    keep = lax.broadcasted_iota(jnp.int32, (SUBLANES, HG_K), 0) < tn
    q, logf, k = _hgrn2_gates(qb_ref[0], fb_ref[0], layer, lb_ref[...])
    logf = jnp.where(keep, logf, 0.0)
    k = jnp.where(keep, k, 0.0)
    o, st = _gla_chunk(q, k, ib_ref[0], logf, s0_ref[0, 0].T, group=SUBLANES)
    s_ref[0, 0] = st.T
    o_ref[0] = _hgrn2_out(o, gb_ref[0], gain_ref[layer:layer + 1, :]).astype(o_ref.dtype)


def _hgrn2_decode(proj, s0, lower_logits, gain, *, layer, heads, tn):
    db, tp, _ = proj.shape
    assert tn <= tp == SUBLANES
    depth = lower_logits.shape[0]

    def col(j):
        return pl.BlockSpec((1, tp, HG_K), lambda b, h: (b, 0, j * heads + h))

    sspec = pl.BlockSpec((1, 1, HG_K, HG_V), lambda b, h: (b, h, 0, 0))
    return pl.pallas_call(
        functools.partial(_hgrn2_decode_body, layer=layer, tn=tn),
        out_shape=(jax.ShapeDtypeStruct((db, tp, heads * HG_V), jnp.bfloat16),
                   jax.ShapeDtypeStruct(s0.shape, jnp.float32)),
        grid=(db, heads),
        in_specs=[col(0), col(1), col(2), col(3), sspec,
                  pl.BlockSpec((depth, HG_K), lambda b, h: (0, h)),
                  pl.BlockSpec((depth, HG_V), lambda b, h: (0, 0))],
        out_specs=(pl.BlockSpec((1, tp, HG_V), lambda b, h: (b, 0, h)), sspec),
        compiler_params=_compiler_params(("parallel", "parallel"), 16 * 1024 * 1024),
        name="hgrn2_decode",
    )(proj, proj, proj, proj, s0, lower_logits, gain)


def _trunk_layer(x, layer, weights, moba, hgrn2, kv_stack, *, a_width, b_width):
    w_in, w_ba, w_bb, w_out, w_up, w_down, g_mix, g_mlp = weights
    d = x[0].shape[1]
    f32, bf16 = jnp.float32, jnp.bfloat16
    h = [_rmsnorm(xi, g_mix[layer], bf16) for xi in x]
    proj = functools.partial(_matmul, h[0], h[1], w_in, layer=layer)
    qa = proj(col0=0, ncols=a_width, out_dtype=f32, name="proj_q")
    ka = proj(col0=a_width, ncols=a_width, out_dtype=f32, stack=kv_stack[0], name="proj_k")
    va = proj(col0=2 * a_width, ncols=a_width, out_dtype=f32, stack=kv_stack[1], name="proj_v")
    hg = proj(col0=3 * a_width, ncols=4 * b_width, out_dtype=f32, name="proj_hgrn")
    gates = proj(col0=3 * a_width + 4 * b_width, ncols=2 * d, out_dtype=bf16, act=_act_sigmoid,
                 name="proj_gates")
    o_a = [f(q, k, v) for f, q, k, v in zip(moba, qa, ka, va)]
    o_b, s_new = zip(*[f(g) for f, g in zip(hgrn2, hg)])
    merged = _merge_branches(o_a, o_b, w_ba, w_bb, gates, layer=layer)
    x = _matmul(*merged, w_out, layer=layer, out_dtype=f32, residual=x, name="out_proj")
    h2 = [_rmsnorm(xi, g_mlp[layer], bf16) for xi in x]
    u = _matmul(*h2, w_up, layer=layer, out_dtype=bf16, act=_act_relu2, name="ffn_up")
    x = _matmul(*u, w_down, layer=layer, out_dtype=f32, residual=x, name="ffn_down")
    return x, ka, va, s_new


def kernel(x_prompt, x_sample, cache_k, cache_v, state_hgrn, page_table, w_in, w_branch_a,
           w_branch_b, w_out, w_up, w_down, norm_mix, norm_mlp, hg_norm, hg_lower_bounds,
           rel_bias, final_norm):
    batch, seq, d = x_prompt.shape
    db, tn, _ = x_sample.shape
    depth = w_in.shape[0]
    a_width = w_branch_a.shape[1]
    b_width = w_branch_b.shape[1]
    a_heads = a_width // HEAD_DIM
    b_heads = b_width // HG_V
    n_pages = page_table.shape[1]
    n_full = (n_pages * PAGE_SIZE) // MOBA_BLOCK
    assert (n_pages * PAGE_SIZE) % MOBA_BLOCK == 0
    n_top = min(MOBA_TOPK, n_full)
    assert n_top > 0 and tn <= SUBLANES

    bf16 = jnp.bfloat16
    bias_t = rel_bias.astype(jnp.float32).T
    tiles = _bias_tiles(bias_t)
    kmeans = _page_block_means(cache_k, page_table)

    x = (x_prompt.reshape(batch * seq, d), x_sample.reshape(db * tn, d))
    weights = (w_in.astype(bf16), w_branch_a.astype(bf16), w_branch_b.astype(bf16),
               w_out.astype(bf16), w_up.astype(bf16), w_down.astype(bf16), norm_mix, norm_mlp)
    outs = {name: [] for name in ("sp", "ks", "vs", "ss")}
    kp = jnp.zeros((depth, batch * seq, a_width), jnp.float32)
    vp = jnp.zeros((depth, batch * seq, a_width), jnp.float32)
    for l in range(depth):

        def moba_p(q, k, v, l=l):
            return _moba_prompt(q, k, v, bias_t, tiles, layer=l, batch=batch, seq=seq)

        def hgrn_p(hg, l=l):
            return _hgrn2_prompt(hg, hg_lower_bounds, hg_norm, layer=l, batch=batch, seq=seq,
                                 heads=b_heads)

        def pad_tokens(a):
            return jnp.pad(a.reshape(db, tn, -1), ((0, 0), (0, SUBLANES - tn), (0, 0)))

        def moba_s(q, k, v, l=l):
            q3, k3, v3 = pad_tokens(q), pad_tokens(k), pad_tokens(v)
            top = _decode_topk(q3, kmeans, layer=l, n_top=n_top)[:, :, :tn, :n_top]
            o = _moba_decode(q3, k3, v3, cache_k, cache_v, page_table, top, bias_t, layer=l,
                             heads=a_heads)
            return o[:, :tn].reshape(db * tn, a_width)

        def hgrn_s(hg, l=l):
            o, s = _hgrn2_decode(pad_tokens(hg), state_hgrn[l], hg_lower_bounds, hg_norm,
                                 layer=l, heads=b_heads, tn=tn)
            return o[:, :tn].reshape(db * tn, b_width), s

        x, (kp, ks), (vp, vs), (sp, ss) = _trunk_layer(
            x, l, weights, (moba_p, moba_s), (hgrn_p, hgrn_s), (kp, vp),
            a_width=a_width, b_width=b_width)
        outs["sp"].append(sp)
        outs["ks"].append(ks.reshape(db, tn, a_heads, HEAD_DIM))
        outs["vs"].append(vs.reshape(db, tn, a_heads, HEAD_DIM))
        outs["ss"].append(ss)

    y_prompt = _rmsnorm(x[0], final_norm, jnp.float32).reshape(batch, seq, d)
    y_sample = _rmsnorm(x[1], final_norm, jnp.float32).reshape(db, tn, d)
    kv_shape = (depth, batch, seq, a_heads, HEAD_DIM)
    return (y_prompt, y_sample, kp.reshape(kv_shape), vp.reshape(kv_shape),
            jnp.stack(outs["sp"]), jnp.stack(outs["ks"]), jnp.stack(outs["vs"]),
            jnp.stack(outs["ss"]))
```

```python
import functools
import math

import jax
import jax.numpy as jnp
from jax import lax
from jax.experimental import pallas as pl
from jax.experimental.pallas import tpu as pltpu

HEAD_DIM = 128
MOBA_BLOCK = 256
MOBA_TOPK = 3
NUM_BUCKETS = 32
MAX_DISTANCE = 128
PAGE_SIZE = 128
HG_K = 128
HG_V = 128
EPS = 1e-6

V7X_VMEM_LIMIT_BYTES = 60000 * 1024
SUBLANES = 8
LANES = 128

GLA_CHUNK = 256
GLA_GROUP = 4
LOG2_E = math.log2(math.e)
NEG_INF = float("-inf")


def _compiler_params(semantics, vmem_bytes):
    limit = int(min(max(vmem_bytes, 16 * 1024 * 1024), V7X_VMEM_LIMIT_BYTES))
    return pltpu.CompilerParams(dimension_semantics=semantics, vmem_limit_bytes=limit)


def _rmsnorm_body(x_ref, g_ref, o_ref):
    x = x_ref[...]
    ms = jnp.mean(x * x, axis=-1, keepdims=True)
    o_ref[...] = (x * lax.rsqrt(ms + EPS) * g_ref[...]).astype(o_ref.dtype)


def _rmsnorm(x, gain, out_dtype):
    m, d = x.shape
    tm = min(m, 256)
    assert m % tm == 0
    block_bytes = tm * d * (4 + jnp.dtype(out_dtype).itemsize)
    return pl.pallas_call(
        _rmsnorm_body,
        out_shape=jax.ShapeDtypeStruct((m, d), out_dtype),
        grid=(m // tm,),
        in_specs=[pl.BlockSpec((tm, d), lambda i: (i, 0)),
                  pl.BlockSpec((1, d), lambda i: (0, 0))],
        out_specs=pl.BlockSpec((tm, d), lambda i: (i, 0)),
        compiler_params=_compiler_params(("parallel",), 4 * block_bytes),
        name="rmsnorm",
    )(x, gain.reshape(1, d))


def _act_none(a):
    return a


def _act_sigmoid(a):
    return jax.nn.sigmoid(a)


def _act_relu2(a):
    return jnp.square(jnp.maximum(a, 0.0))


def _mm_body(xp_ref, xs_ref, w_ref, *refs, act, has_res, nk):
    op_ref, os_ref = refs[-2:]
    rp_ref, rs_ref = refs[:2] if has_res else (None, None)
    i = pl.program_id(1)
    k = pl.program_id(2)

    def emit(x_ref, r_ref, o_ref):
        if nk == 1:
            acc = jnp.dot(x_ref[...], w_ref[...], preferred_element_type=jnp.float32)
            if has_res:
                acc = acc + r_ref[...]
            o_ref[...] = act(acc).astype(o_ref.dtype)
        else:
            @pl.when(k == 0)
            def _():
                o_ref[...] = r_ref[...] if has_res else jnp.zeros_like(o_ref)

            o_ref[...] += jnp.dot(x_ref[...], w_ref[...], preferred_element_type=jnp.float32)

    emit(xp_ref, rp_ref, op_ref)

    @pl.when(i == 0)
    def _():
        emit(xs_ref, rs_ref, os_ref)


def _matmul(xp, xs, w, *, col0=0, ncols=None, out_dtype, act=_act_none, residual=None,
            stack=None, name):
    m, kdim = xp.shape
    ms = xs.shape[0]
    ncols = w.shape[1] if ncols is None else ncols
    tm = min(m, 1024)
    tk = min(kdim, 4096)
    tn = math.gcd(ncols, col0, 1024)
    assert m % tm == 0 and kdim % tk == 0 and ncols % tn == 0 and col0 % tn == 0
    nm, nn, nk = m // tm, ncols // tn, kdim // tk
    assert nk == 1 or (out_dtype == jnp.float32 and act is _act_none)
    c0 = col0 // tn
    in_specs = [pl.BlockSpec((tm, tk), lambda n, i, k: (i, k)),
                pl.BlockSpec((ms, tk), lambda n, i, k: (0, k)),
                pl.BlockSpec((tk, tn), lambda n, i, k: (k, n + c0))]
    args = [xp, xs, w]
    osize = jnp.dtype(out_dtype).itemsize
    vmem = (2 * ((tm + ms) * tk * 2 + tk * tn * 2 + (tm + ms) * tn * osize)
            + 2 * (tm + ms) * tn * 4)
    if residual is not None:
        in_specs += [pl.BlockSpec((tm, tn), lambda n, i, k: (i, n)),
                     pl.BlockSpec((ms, tn), lambda n, i, k: (0, n))]
        args += list(residual)
        vmem += 2 * (tm + ms) * tn * 4
    aliases = {}
    if stack is None:
        p_shape = jax.ShapeDtypeStruct((m, ncols), out_dtype)
        p_spec = pl.BlockSpec((tm, tn), lambda n, i, k: (i, n))
    else:
        buf, layer = stack
        assert buf.shape[1:] == (m, ncols) and buf.dtype == out_dtype
        p_shape = jax.ShapeDtypeStruct(buf.shape, out_dtype)
        p_spec = pl.BlockSpec((None, tm, tn), lambda n, i, k: (layer, i, n))
        in_specs.append(pl.BlockSpec(memory_space=pl.ANY))
        args.append(buf)
        aliases = {len(args) - 1: 0}
    return pl.pallas_call(
        functools.partial(_mm_body, act=act, has_res=residual is not None, nk=nk),
        out_shape=(p_shape, jax.ShapeDtypeStruct((ms, ncols), out_dtype)),
        grid=(nn, nm, nk),
        in_specs=in_specs,
        out_specs=(p_spec, pl.BlockSpec((ms, tn), lambda n, i, k: (0, n))),
        input_output_aliases=aliases,
        compiler_params=_compiler_params(("arbitrary", "arbitrary", "arbitrary"),
                                         vmem + 4 * 1024 * 1024),
        name=name,
    )(*args)


def _merge_body(oap_ref, obp_ref, oas_ref, obs_ref, wa_ref, wb_ref, gap_ref, gbp_ref, gas_ref,
                gbs_ref, op_ref, os_ref):
    def emit(oa_ref, ob_ref, ga_ref, gb_ref, o_ref):
        a = jnp.dot(oa_ref[...], wa_ref[...], preferred_element_type=jnp.float32)
        b = jnp.dot(ob_ref[...], wb_ref[...], preferred_element_type=jnp.float32)
        o_ref[...] = (ga_ref[...].astype(jnp.float32) * a
                      + gb_ref[...].astype(jnp.float32) * b).astype(o_ref.dtype)

    emit(oap_ref, obp_ref, gap_ref, gbp_ref, op_ref)

    @pl.when(pl.program_id(1) == 0)
    def _():
        emit(oas_ref, obs_ref, gas_ref, gbs_ref, os_ref)


def _merge_branches(o_a, o_b, w_ba, w_bb, gates):
    (oap, oas), (obp, obs), (gp, gs) = o_a, o_b, gates
    m, ka = oap.shape
    ms = oas.shape[0]
    kb = obp.shape[1]
    d = w_ba.shape[1]
    tm = min(m, 1024)
    tn = min(d, 512)
    assert m % tm == 0 and d % tn == 0
    goff = d // tn
    rows = tm + ms
    vmem = 2 * 2 * (rows * ka + rows * kb + ka * tn + kb * tn + 3 * rows * tn) + 3 * rows * tn * 4
    return pl.pallas_call(
        _merge_body,
        out_shape=(jax.ShapeDtypeStruct((m, d), jnp.bfloat16),
                   jax.ShapeDtypeStruct((ms, d), jnp.bfloat16)),
        grid=(d // tn, m // tm),
        in_specs=[pl.BlockSpec((tm, ka), lambda n, i: (i, 0)),
                  pl.BlockSpec((tm, kb), lambda n, i: (i, 0)),
                  pl.BlockSpec((ms, ka), lambda n, i: (0, 0)),
                  pl.BlockSpec((ms, kb), lambda n, i: (0, 0)),
                  pl.BlockSpec((ka, tn), lambda n, i: (0, n)),
                  pl.BlockSpec((kb, tn), lambda n, i: (0, n)),
                  pl.BlockSpec((tm, tn), lambda n, i: (i, n)),
                  pl.BlockSpec((tm, tn), lambda n, i: (i, n + goff)),
                  pl.BlockSpec((ms, tn), lambda n, i: (0, n)),
                  pl.BlockSpec((ms, tn), lambda n, i: (0, n + goff))],
        out_specs=(pl.BlockSpec((tm, tn), lambda n, i: (i, n)),
                   pl.BlockSpec((ms, tn), lambda n, i: (0, n))),
        compiler_params=_compiler_params(("arbitrary", "arbitrary"), vmem + 4 * 1024 * 1024),
        name="merge_branches",
    )(oap, obp, oas, obs, w_ba, w_bb, gp, gp, gs, gs)


def _cast_plan(casts, steps, step_of):
    in_specs, out_specs, out_shapes, args, vmem = [], [], [], [], 0
    for w, layer in casts:
        _, r, c = w.shape
        rows = r // steps
        assert r % steps == 0 and rows % (2 * SUBLANES) == 0
        in_specs.append(pl.BlockSpec((None, rows, c),
                                     lambda *g, layer=layer: (layer, step_of(*g), 0)))
        out_specs.append(pl.BlockSpec((rows, c), lambda *g: (step_of(*g), 0)))
        out_shapes.append(jax.ShapeDtypeStruct((r, c), jnp.bfloat16))
        args.append(w)
        vmem += 2 * rows * c * (4 + 2)
    return in_specs, out_specs, out_shapes, args, vmem


def _emit_casts(src_refs, dst_refs):
    for src, dst in zip(src_refs, dst_refs):
        dst[...] = src[...].astype(dst.dtype)


def _t5_bias(dist, bias_ref, head):
    max_exact = NUM_BUCKETS // 2
    nf = jnp.maximum(dist, 1).astype(jnp.float32)
    far = max_exact + (jnp.log(nf / max_exact) / math.log(MAX_DISTANCE / max_exact)
                       * (NUM_BUCKETS - max_exact)).astype(jnp.int32)
    far = jnp.minimum(far, NUM_BUCKETS - 1)
    bucket = jnp.where(dist < max_exact, dist, far)
    out = jnp.zeros(dist.shape, jnp.float32)
    for k in range(NUM_BUCKETS):
        out = jnp.where(bucket == k, bias_ref[head, k], out)
    return out


def _bias_tiles_body(bias_ref, o_ref):
    h = pl.program_id(0)
    row = lax.broadcasted_iota(jnp.int32, (MOBA_BLOCK, MOBA_BLOCK), 0)
    col = lax.broadcasted_iota(jnp.int32, (MOBA_BLOCK, MOBA_BLOCK), 1)
    own = _t5_bias(jnp.maximum(row - col, 0), bias_ref, h)
    o_ref[0, 0] = jnp.where(row >= col, own, NEG_INF)
    o_ref[0, 1] = _t5_bias(row - col + MOBA_BLOCK, bias_ref, h)


def _bias_tiles(bias_t):
    h = bias_t.shape[0]
    return pl.pallas_call(
        _bias_tiles_body,
        out_shape=jax.ShapeDtypeStruct((h, 2, MOBA_BLOCK, MOBA_BLOCK), jnp.float32),
        grid_spec=pltpu.PrefetchScalarGridSpec(
            num_scalar_prefetch=1,
            grid=(h,),
            in_specs=[],
            out_specs=pl.BlockSpec((1, 2, MOBA_BLOCK, MOBA_BLOCK), lambda i, b: (i, 0, 0, 0)),
        ),
        compiler_params=_compiler_params(("arbitrary",), 8 * 1024 * 1024),
        name="t5_bias_tiles",
    )(bias_t)


def _dot_nt(a, b, precision=None):
    return lax.dot_general(a, b, (((1,), (1,)), ((), ())), precision=precision,
                           preferred_element_type=jnp.float32)


def _moba_prompt_body(bias_ref, q_ref, k_ref, v_ref, tiles_ref, *refs, n_blk):
    n_cast = (len(refs) - 1) // 2
    o_ref = refs[n_cast]
    _emit_casts(refs[:n_cast], refs[n_cast + 1:])
    h = pl.program_id(0)
    t = n_blk * MOBA_BLOCK
    n_top = min(MOBA_TOPK, n_blk - 1)
    scale = HEAD_DIM ** -0.5
    q = q_ref[...]
    k = k_ref[...]
    qb = (q * scale).astype(jnp.bfloat16)
    kb = k.astype(jnp.bfloat16)
    vb = v_ref[...].astype(jnp.bfloat16)
    own_tile = tiles_ref[0, 0]
    adj_tile = tiles_ref[0, 1]

    if n_top > 0:
        kmean = jnp.mean(k.reshape(n_blk, MOBA_BLOCK, HEAD_DIM), axis=1)
        gate = _dot_nt(kmean, q, precision=lax.Precision.HIGHEST)
        nidx = lax.broadcasted_iota(jnp.int32, (n_blk, t), 0)
        qblk = lax.broadcasted_iota(jnp.int32, (n_blk, t), 1) // MOBA_BLOCK
        rank = jnp.zeros((n_blk, t), jnp.int32)
        for n2 in range(n_blk):
            g2 = gate[n2:n2 + 1, :]
            beats = (n2 < qblk) & ((g2 > gate) | ((g2 == gate) & (n2 < nidx)))
            rank = rank + beats.astype(jnp.int32)
        sel_t = ((nidx < qblk) & (rank < n_top)).astype(jnp.float32)
        pad = jnp.zeros((LANES - n_blk, t), jnp.float32)
        sel = jnp.concatenate([sel_t, pad], axis=0).T
        far_bias = bias_ref[h, NUM_BUCKETS - 1]

    for j in range(n_blk):
        r0 = j * MOBA_BLOCK
        nk = r0 + MOBA_BLOCK
        if j == 0 or n_top == 0:
            s = _dot_nt(qb[r0:nk], kb[r0:nk]) + own_tile
            vals = vb[r0:nk]
        else:
            s = _dot_nt(qb[r0:nk], kb[:nk])
            parts = []
            for n in range(j):
                near = n == j - 1
                gate_col = jnp.where(sel[r0:nk, n:n + 1] > 0.5, 0.0 if near else far_bias, NEG_INF)
                tile = s[:, n * MOBA_BLOCK:(n + 1) * MOBA_BLOCK] + gate_col
                parts.append(tile + adj_tile if near else tile)
            parts.append(s[:, r0:nk] + own_tile)
            s = jnp.concatenate(parts, axis=1)
            vals = vb[:nk]
        m = jnp.max(s, axis=1, keepdims=True)
        p = jnp.exp(s - m)
        l = jnp.sum(p, axis=1, keepdims=True)
        o = jnp.dot(p.astype(jnp.bfloat16), vals, preferred_element_type=jnp.float32)
        o_ref[r0:nk, :] = (o / l).astype(o_ref.dtype)


def _moba_prompt(q, k, v, bias_t, tiles, *, layer, batch, seq, casts=()):
    m, width = q.shape
    heads = width // HEAD_DIM
    assert seq % MOBA_BLOCK == 0 and m == batch * seq
    n_blk = seq // MOBA_BLOCK
    spec = pl.BlockSpec((seq, HEAD_DIM), lambda h, b, bias: (b, h))
    kv_spec = pl.BlockSpec((None, seq, HEAD_DIM), lambda h, b, bias: (layer, b, h))
    c_in, c_out, c_shapes, c_args, c_vmem = _cast_plan(
        casts, heads * batch, lambda h, b, bias: h * batch + b)
    vmem = 2 * (3 * seq * HEAD_DIM * 4 + seq * HEAD_DIM * 2) + 8 * MOBA_BLOCK * seq * 4 + c_vmem
    return pl.pallas_call(
        functools.partial(_moba_prompt_body, n_blk=n_blk),
        out_shape=[jax.ShapeDtypeStruct((m, width), jnp.bfloat16)] + c_shapes,
        grid_spec=pltpu.PrefetchScalarGridSpec(
            num_scalar_prefetch=1,
            grid=(heads, batch),
            in_specs=[spec, kv_spec, kv_spec,
                      pl.BlockSpec((1, 2, MOBA_BLOCK, MOBA_BLOCK),
                                   lambda h, b, bias: (h, 0, 0, 0))] + c_in,
            out_specs=[spec] + c_out,
        ),
        compiler_params=_compiler_params(("arbitrary", "arbitrary"), vmem + 8 * 1024 * 1024),
        name="moba_prompt",
    )(bias_t, q, k, v, tiles, *c_args)


PAGES_PER_STEP = 8


def _page_mean_body(pt_ref, *refs, ppb):
    o_ref = refs[-1]
    i = pl.program_id(2)
    blocks = (len(refs) - 1) // ppb
    for j in range(blocks):
        total = sum(jnp.sum(refs[j * ppb + p][0, 0], axis=0) for p in range(ppb))
        o_ref[0, 0, i * blocks + j] = total * (1.0 / MOBA_BLOCK)


def _page_block_means(cache_k, page_table):
    depth, _, page, heads, hd = cache_k.shape
    db, n_pages = page_table.shape
    ppb = MOBA_BLOCK // PAGE_SIZE
    assert page == PAGE_SIZE and n_pages % ppb == 0
    n_full = n_pages // ppb
    pages = math.gcd(n_pages, PAGES_PER_STEP)
    assert pages % ppb == 0

    def page_spec(j):
        return pl.BlockSpec((1, 1, page, heads, hd),
                            lambda l, b, i, pt: (l, pt[b, i * pages + j], 0, 0, 0))

    return pl.pallas_call(
        functools.partial(_page_mean_body, ppb=ppb),
        out_shape=jax.ShapeDtypeStruct((depth, db, n_full, heads, hd), jnp.float32),
        grid_spec=pltpu.PrefetchScalarGridSpec(
            num_scalar_prefetch=1,
            grid=(depth, db, n_pages // pages),
            in_specs=[page_spec(j) for j in range(pages)],
            out_specs=pl.BlockSpec((1, 1, n_full, heads, hd), lambda l, b, i, pt: (l, b, 0, 0, 0)),
        ),
        compiler_params=_compiler_params(("arbitrary",) * 3,
                                         (2 * pages + 4) * page * heads * hd * 4),
        name="page_block_means",
    )(page_table, *([cache_k] * pages))


def _decode_topk_body(q_ref, km_ref, o_ref, *, heads, n_top):
    tn = q_ref.shape[1]
    n_full = km_ref.shape[1]
    lane = lax.broadcasted_iota(jnp.int32, (tn, LANES), 1)
    col = lax.broadcasted_iota(jnp.int32, (tn, n_full), 1).astype(jnp.float32)
    for h in range(heads):
        c0 = h * HEAD_DIM
        gate = _dot_nt(q_ref[0, :, c0:c0 + HEAD_DIM], km_ref[0, :, h, :],
                       precision=lax.Precision.HIGHEST)
        out = jnp.zeros((tn, LANES), jnp.int32)
        for r in range(n_top):
            best = jnp.max(gate, axis=1, keepdims=True)
            idx = jnp.min(jnp.where(gate == best, col, float(n_full)), axis=1, keepdims=True)
            out = jnp.where(lane == r, idx.astype(jnp.int32), out)
            gate = jnp.where(col == idx, NEG_INF, gate)
        o_ref[0, h] = out


def _decode_topk(q, kmeans, *, layer, n_top):
    db, tn, width = q.shape
    _, _, n_full, heads, hd = kmeans.shape
    return pl.pallas_call(
        functools.partial(_decode_topk_body, heads=heads, n_top=n_top),
        out_shape=jax.ShapeDtypeStruct((db, heads, tn, LANES), jnp.int32),
        grid=(db,),
        in_specs=[pl.BlockSpec((1, tn, width), lambda b: (b, 0, 0)),
                  pl.BlockSpec((None, 1, n_full, heads, hd), lambda b: (layer, b, 0, 0, 0))],
        out_specs=pl.BlockSpec((1, heads, tn, LANES), lambda b: (b, 0, 0, 0)),
        compiler_params=_compiler_params(("parallel",), 16 * 1024 * 1024),
        name="decode_topk",
    )(q, kmeans)


def _moba_decode_body(top_ref, pt_ref, bias_ref, q_ref, k_ref, v_ref, ck_ref, cv_ref, o_ref,
                      kbuf, vbuf, sems, *, layer, heads, tn, n_top, past):
    step = pl.program_id(0)
    n_steps = pl.num_programs(0)
    ppb = MOBA_BLOCK // PAGE_SIZE
    n_sel = tn * n_top * MOBA_BLOCK
    own_rows = LANES

    def gather(s, slot, start):
        b = s // heads
        h = s % heads
        for t in range(tn):
            for r in range(n_top):
                blk = top_ref[((b * heads + h) * tn + t) * n_top + r]
                for p in range(ppb):
                    phys = pt_ref[b, blk * ppb + p]
                    row0 = (t * n_top + r) * MOBA_BLOCK + p * PAGE_SIZE
                    for src, dst, sem in ((ck_ref, kbuf, sems.at[slot, 0]),
                                          (cv_ref, vbuf, sems.at[slot, 1])):
                        cp = pltpu.make_async_copy(
                            src.at[layer, phys, :, h, :],
                            dst.at[slot, pl.ds(row0, PAGE_SIZE), :], sem)
                        if start:
                            cp.start()
                        else:
                            cp.wait()

    slot = step % 2

    @pl.when(step == 0)
    def _():
        gather(step, 0, True)

    @pl.when(step + 1 < n_steps)
    def _():
        gather(step + 1, 1 - slot, True)

    gather(step, slot, False)

    b = step // heads
    h = step % heads
    scale = HEAD_DIM ** -0.5
    tp = q_ref.shape[1]
    qb = q_ref[0].astype(jnp.bfloat16)
    zeros = jnp.zeros((own_rows - tp, HEAD_DIM), jnp.float32)
    k_all = jnp.concatenate([kbuf[slot], k_ref[0], zeros], axis=0).astype(jnp.bfloat16)
    v_all = jnp.concatenate([vbuf[slot], v_ref[0], zeros], axis=0).astype(jnp.bfloat16)
    n_cols = n_sel + own_rows
    s = _dot_nt(qb, k_all) * scale

    row = lax.broadcasted_iota(jnp.int32, (tp, n_cols), 0)
    col = lax.broadcasted_iota(jnp.int32, (tp, n_cols), 1)
    seg = col // MOBA_BLOCK
    key_pos = past + (col - n_sel)
    for t in range(tn):
        for r in range(n_top):
            blk = top_ref[((b * heads + h) * tn + t) * n_top + r]
            key_pos = jnp.where(seg == t * n_top + r,
                                blk * MOBA_BLOCK + col % MOBA_BLOCK, key_pos)
    q_pos = past + row
    own = col >= n_sel
    valid = (own & ((col - n_sel) <= row)) | (jnp.logical_not(own) & (seg // n_top == row))
    bias = _t5_bias(jnp.maximum(q_pos - key_pos, 0), bias_ref, h)
    s = jnp.where(valid, s + bias, NEG_INF)
    m = jnp.max(s, axis=1, keepdims=True)
    p = jnp.exp(s - m)
    l = jnp.sum(p, axis=1, keepdims=True)
    o = jnp.dot(p.astype(jnp.bfloat16), v_all, preferred_element_type=jnp.float32)
    o_ref[0] = (o / l).astype(o_ref.dtype)


def _moba_decode(q, k, v, cache_k, cache_v, page_table, top_idx, bias_t, *, layer, heads):
    db, tp, width = q.shape
    tn, n_top = top_idx.shape[-2:]
    past = page_table.shape[1] * PAGE_SIZE
    n_sel = tn * n_top * MOBA_BLOCK
    spec = pl.BlockSpec((1, tp, HEAD_DIM), lambda s, *_: (s // heads, 0, s % heads))
    any_spec = pl.BlockSpec(memory_space=pl.ANY)
    vmem = 2 * 2 * n_sel * HEAD_DIM * 4 + 6 * (n_sel + LANES) * HEAD_DIM * 4
    return pl.pallas_call(
        functools.partial(_moba_decode_body, layer=layer, heads=heads, tn=tn, n_top=n_top,
                          past=past),
        out_shape=jax.ShapeDtypeStruct((db, tp, width), jnp.bfloat16),
        grid_spec=pltpu.PrefetchScalarGridSpec(
            num_scalar_prefetch=3,
            grid=(db * heads,),
            in_specs=[spec, spec, spec, any_spec, any_spec],
            out_specs=spec,
            scratch_shapes=[pltpu.VMEM((2, n_sel, HEAD_DIM), jnp.float32),
                            pltpu.VMEM((2, n_sel, HEAD_DIM), jnp.float32),
                            pltpu.SemaphoreType.DMA((2, 2))],
        ),
        compiler_params=_compiler_params(("arbitrary",), vmem + 8 * 1024 * 1024),
        name="moba_decode",
    )(top_idx.reshape(-1), page_table, bias_t, q, k, v, cache_k, cache_v)


def _hgrn2_gates(qb, fb, layer, lbs):
    q = _silu(qb)
    e = jnp.exp(-jnp.abs(fb))
    t = 1.0 / (1.0 + e)
    et = e * t
    pos = fb >= 0.0
    sig_neg = jnp.where(pos, et, t)
    if layer == 0:
        logf = jnp.minimum(fb, 0.0) - jnp.log(1.0 + e)
        k = sig_neg
    else:
        w = jnp.exp(lbs - jnp.max(lbs, axis=0, keepdims=True))
        p = w / jnp.sum(w, axis=0, keepdims=True)
        lb = jnp.sum(p[:layer + 1], axis=0, keepdims=True) - p[0:1]
        logf = jnp.log(lb + (1.0 - lb) * jnp.where(pos, t, et))
        k = (1.0 - lb) * sig_neg
    return q, logf, k


def _gla_level_ids(c, group):
    t = lax.broadcasted_iota(jnp.int32, (c, c), 0)
    s = lax.broadcasted_iota(jnp.int32, (c, c), 1)
    ids = jnp.full((c, c), -1, jnp.int32)
    size, j = 2 * group, 0
    while size <= c:
        half = size // 2
        hit = (t // size == s // size) & (t % size >= half) & (s % size < half)
        ids = jnp.where(hit, j, ids)
        size, j = 2 * size, j + 1
    return ids


def _gla_chunk(q, k, v, g, st, *, group, tril=None, level_ids=None):
    c, kd = q.shape
    g = g * LOG2_E
    if tril is None:
        row = lax.broadcasted_iota(jnp.int32, q.shape, 0)
        b = g
        sh = 1
        while sh < c:
            b = b + jnp.where(row >= sh, pltpu.roll(b, sh, 0), 0.0)
            sh *= 2
    else:
        hi = g.astype(jnp.bfloat16)
        r1 = g - hi.astype(jnp.float32)
        mid = r1.astype(jnp.bfloat16)
        lo = (r1 - mid.astype(jnp.float32)).astype(jnp.bfloat16)
        b = (jnp.dot(tril, hi, preferred_element_type=jnp.float32)
             + jnp.dot(tril, mid, preferred_element_type=jnp.float32)
             + jnp.dot(tril, lo, preferred_element_type=jnp.float32))

    shape3 = (c // SUBLANES, SUBLANES, kd)
    q3, k3, v3, b3 = (a.reshape(shape3) for a in (q, k, v, b))
    sub = lax.broadcasted_iota(jnp.int32, (1, SUBLANES, kd), 1) % group
    o3 = jnp.sum(q3 * k3, axis=2, keepdims=True) * v3
    for d in range(1, group):
        decay = jnp.exp2(jnp.where(sub >= d, b3 - pltpu.roll(b3, d, 1), NEG_INF))
        w = jnp.sum(q3 * decay * pltpu.roll(k3, d, 1), axis=2, keepdims=True)
        o3 = o3 + w * pltpu.roll(v3, d, 1)
    o = o3.reshape(c, v.shape[1])

    if c > group:
        row = lax.broadcasted_iota(jnp.int32, q.shape, 0)
        scores = jnp.zeros((c, c), jnp.float32)
        size, j = 2 * group, 0
        while size <= c:
            half = size // 2
            bs = b.reshape(c // size, size, kd)
            ref = jnp.broadcast_to(bs[:, half - 1:half, :], bs.shape).reshape(b.shape)
            upper = (row % size) >= half
            diff = b - ref
            x = (jnp.where(upper, q, k) * jnp.exp2(jnp.where(upper, diff, -diff))
                 ).astype(jnp.bfloat16)
            scores = jnp.where(level_ids == j, _dot_nt(x, x), scores)
            size, j = 2 * size, j + 1
        o = o + jnp.dot(scores.astype(jnp.bfloat16), v.astype(jnp.bfloat16),
                        preferred_element_type=jnp.float32)

    b_last = b[c - 1:c, :]
    o = o + _dot_nt((q * jnp.exp2(b)).astype(jnp.bfloat16), st.astype(jnp.bfloat16))
    kw = (k * jnp.exp2(b_last - b)).astype(jnp.bfloat16)
    st_new = st * jnp.exp2(b_last) + lax.dot_general(
        v.astype(jnp.bfloat16), kw, (((0,), (0,)), ((), ())), preferred_element_type=jnp.float32)
    return o, st_new


def _silu(x):
    return x / (1.0 + jnp.exp(-x))


def _hgrn2_out(o, gate, gain):
    ms = jnp.mean(o * o, axis=-1, keepdims=True)
    return o * lax.rsqrt(ms + EPS) * gain * _silu(gate)


def _hgrn2_prompt_body(qb_ref, fb_ref, ib_ref, gb_ref, lb_ref, gain_ref, *refs, layer, n_chunks):
    n_cast = (len(refs) - 3) // 2
    o_ref, s_ref = refs[n_cast:n_cast + 2]
    st_ref = refs[-1]
    _emit_casts(refs[:n_cast], refs[n_cast + 2:-1])
    st_ref[...] = jnp.zeros_like(st_ref)
    lbs = lb_ref[...]
    gain = gain_ref[layer:layer + 1, :]
    tril = (lax.broadcasted_iota(jnp.int32, (GLA_CHUNK, GLA_CHUNK), 0)
            >= lax.broadcasted_iota(jnp.int32, (GLA_CHUNK, GLA_CHUNK), 1)).astype(jnp.bfloat16)
    level_ids = _gla_level_ids(GLA_CHUNK, GLA_GROUP)

    def chunk(i, carry):
        rows = pl.ds(pl.multiple_of(i * GLA_CHUNK, GLA_CHUNK), GLA_CHUNK)
        q, logf, k = _hgrn2_gates(qb_ref[rows, :], fb_ref[rows, :], layer, lbs)
        o, st = _gla_chunk(q, k, ib_ref[rows, :], logf, st_ref[...], group=GLA_GROUP, tril=tril,
                           level_ids=level_ids)
        st_ref[...] = st
        o_ref[rows, :] = _hgrn2_out(o, gb_ref[rows, :], gain).astype(o_ref.dtype)
        return carry

    lax.fori_loop(0, n_chunks, chunk, 0, unroll=2 if n_chunks % 2 == 0 else 1)
    s_ref[0, 0] = st_ref[...].T


def _hgrn2_prompt(proj, lower_logits, gain, *, layer, batch, seq, heads, casts=()):
    m = proj.shape[0]
    assert seq % GLA_CHUNK == 0 and m == batch * seq
    depth = lower_logits.shape[0]

    def col(j):
        return pl.BlockSpec((seq, HG_K), lambda b, h: (b, j * heads + h))

    par = pl.BlockSpec((depth, HG_K), lambda b, h: (0, h))
    gspec = pl.BlockSpec((depth, HG_V), lambda b, h: (0, 0))
    c_in, c_out, c_shapes, c_args, c_vmem = _cast_plan(
        casts, batch * heads, lambda b, h: b * heads + h)
    vmem = (2 * (4 * seq * HG_K * 4 + seq * HG_V * 2) + 16 * GLA_CHUNK * GLA_CHUNK * 4
            + c_vmem)
    return pl.pallas_call(
        functools.partial(_hgrn2_prompt_body, layer=layer, n_chunks=seq // GLA_CHUNK),
        out_shape=[jax.ShapeDtypeStruct((m, heads * HG_V), jnp.bfloat16),
                   jax.ShapeDtypeStruct((batch, heads, HG_K, HG_V), jnp.float32)] + c_shapes,
        grid=(batch, heads),
        in_specs=[col(0), col(1), col(2), col(3), par, gspec] + c_in,
        out_specs=[pl.BlockSpec((seq, HG_V), lambda b, h: (b, h)),
                   pl.BlockSpec((1, 1, HG_K, HG_V), lambda b, h: (b, h, 0, 0))] + c_out,
        scratch_shapes=[pltpu.VMEM((HG_V, HG_K), jnp.float32)],
        compiler_params=_compiler_params(("arbitrary", "arbitrary"), vmem + 8 * 1024 * 1024),
        name="hgrn2_prompt",
    )(proj, proj, proj, proj, lower_logits, gain, *c_args)


def _hgrn2_decode_body(qb_ref, fb_ref, ib_ref, gb_ref, s0_ref, lb_ref, gain_ref, o_ref, s_ref,
                       *, layer, tn):
    keep = lax.broadcasted_iota(jnp.int32, (SUBLANES, HG_K), 0) < tn
    q, logf, k = _hgrn2_gates(qb_ref[0], fb_ref[0], layer, lb_ref[...])
    logf = jnp.where(keep, logf, 0.0)
    k = jnp.where(keep, k, 0.0)
    o, st = _gla_chunk(q, k, ib_ref[0], logf, s0_ref[0, 0].T, group=SUBLANES)
    s_ref[0, 0] = st.T
    o_ref[0] = _hgrn2_out(o, gb_ref[0], gain_ref[layer:layer + 1, :]).astype(o_ref.dtype)


def _hgrn2_decode(proj, s0, lower_logits, gain, *, layer, heads, tn):
    db, tp, _ = proj.shape
    assert tn <= tp == SUBLANES
    depth = lower_logits.shape[0]

    def col(j):
        return pl.BlockSpec((1, tp, HG_K), lambda b, h: (b, 0, j * heads + h))

    sspec = pl.BlockSpec((1, 1, HG_K, HG_V), lambda b, h: (b, h, 0, 0))
    return pl.pallas_call(
        functools.partial(_hgrn2_decode_body, layer=layer, tn=tn),
        out_shape=(jax.ShapeDtypeStruct((db, tp, heads * HG_V), jnp.bfloat16),
                   jax.ShapeDtypeStruct(s0.shape, jnp.float32)),
        grid=(db, heads),
        in_specs=[col(0), col(1), col(2), col(3), sspec,
                  pl.BlockSpec((depth, HG_K), lambda b, h: (0, h)),
                  pl.BlockSpec((depth, HG_V), lambda b, h: (0, 0))],
        out_specs=(pl.BlockSpec((1, tp, HG_V), lambda b, h: (b, 0, h)), sspec),
        compiler_params=_compiler_params(("parallel", "parallel"), 16 * 1024 * 1024),
        name="hgrn2_decode",
    )(proj, proj, proj, proj, s0, lower_logits, gain)


def _trunk_layer(x, layer, wb, norms, moba, hgrn2, kv_stack, *, a_width, b_width):
    g_mix, g_mlp = norms
    d = x[0].shape[1]
    f32, bf16 = jnp.float32, jnp.bfloat16
    h = [_rmsnorm(xi, g_mix[layer], bf16) for xi in x]
    proj = functools.partial(_matmul, h[0], h[1], wb["in", layer])
    qa = proj(col0=0, ncols=a_width, out_dtype=f32, name="proj_q")
    ka = proj(col0=a_width, ncols=a_width, out_dtype=f32, stack=(kv_stack[0], layer),
              name="proj_k")
    va = proj(col0=2 * a_width, ncols=a_width, out_dtype=f32, stack=(kv_stack[1], layer),
              name="proj_v")
    hg = proj(col0=3 * a_width, ncols=4 * b_width, out_dtype=f32, name="proj_hgrn")
    gates = proj(col0=3 * a_width + 4 * b_width, ncols=2 * d, out_dtype=bf16, act=_act_sigmoid,
                 name="proj_gates")
    o_a = [f(q, k, v) for f, q, k, v in zip(moba, qa, ka, va)]
    o_b, s_new = zip(*[f(g) for f, g in zip(hgrn2, hg)])
    merged = _merge_branches(o_a, o_b, wb["ba", layer], wb["bb", layer], gates)
    x = _matmul(*merged, wb["out", layer], out_dtype=f32, residual=x, name="out_proj")
    h2 = [_rmsnorm(xi, g_mlp[layer], bf16) for xi in x]
    u = _matmul(*h2, wb["up", layer], out_dtype=bf16, act=_act_relu2, name="ffn_up")
    x = _matmul(*u, wb["down", layer], out_dtype=f32, residual=x, name="ffn_down")
    return x, ka, va, s_new


def kernel(x_prompt, x_sample, cache_k, cache_v, state_hgrn, page_table, w_in, w_branch_a,
           w_branch_b, w_out, w_up, w_down, norm_mix, norm_mlp, hg_norm, hg_lower_bounds,
           rel_bias, final_norm):
    batch, seq, d = x_prompt.shape
    db, tn, _ = x_sample.shape
    depth = w_in.shape[0]
    a_width = w_branch_a.shape[1]
    b_width = w_branch_b.shape[1]
    a_heads = a_width // HEAD_DIM
    b_heads = b_width // HG_V
    n_pages = page_table.shape[1]
    n_full = (n_pages * PAGE_SIZE) // MOBA_BLOCK
    assert (n_pages * PAGE_SIZE) % MOBA_BLOCK == 0
    n_top = min(MOBA_TOPK, n_full)
    assert n_top > 0 and tn <= SUBLANES

    bf16 = jnp.bfloat16
    bias_t = rel_bias.astype(jnp.float32).T
    tiles = _bias_tiles(bias_t)
    kmeans = _page_block_means(cache_k, page_table)

    x = (x_prompt.reshape(batch * seq, d), x_sample.reshape(db * tn, d))
    w_f32 = {"in": w_in, "ba": w_branch_a, "bb": w_branch_b, "out": w_out, "up": w_up,
             "down": w_down}
    wb = {(name, 0): w_f32[name][0].astype(bf16) for name in ("in", "ba", "bb", "out")}
    outs = {name: [] for name in ("sp", "ks", "vs", "ss")}
    kp = jnp.zeros((depth, batch * seq, a_width), jnp.float32)
    vp = jnp.zeros((depth, batch * seq, a_width), jnp.float32)
    for l in range(depth):
        nxt = [l + 1] if l + 1 < depth else []

        def with_casts(fn, n_out, keys):
            def run(*args):
                res = fn(*args, casts=[(w_f32[name], lyr) for name, lyr in keys])
                wb.update(zip(keys, res[n_out:]))
                return res[0] if n_out == 1 else tuple(res[:n_out])
            return run

        moba_p = with_casts(
            functools.partial(_moba_prompt, bias_t=bias_t, tiles=tiles, layer=l, batch=batch,
                              seq=seq),
            1, [("up", l)] + [(name, j) for j in nxt for name in ("out", "ba", "bb")])
        hgrn_p = with_casts(
            functools.partial(_hgrn2_prompt, lower_logits=hg_lower_bounds, gain=hg_norm, layer=l,
                              batch=batch, seq=seq, heads=b_heads),
            2, [("down", l)] + [("in", j) for j in nxt])

        def pad_tokens(a):
            return jnp.pad(a.reshape(db, tn, -1), ((0, 0), (0, SUBLANES - tn), (0, 0)))

        def moba_s(q, k, v, l=l):
            q3, k3, v3 = pad_tokens(q), pad_tokens(k), pad_tokens(v)
            top = _decode_topk(q3, kmeans, layer=l, n_top=n_top)[:, :, :tn, :n_top]
            o = _moba_decode(q3, k3, v3, cache_k, cache_v, page_table, top, bias_t, layer=l,
                             heads=a_heads)
            return o[:, :tn].reshape(db * tn, a_width)

        def hgrn_s(hg, l=l):
            o, s = _hgrn2_decode(pad_tokens(hg), state_hgrn[l], hg_lower_bounds, hg_norm,
                                 layer=l, heads=b_heads, tn=tn)
            return o[:, :tn].reshape(db * tn, b_width), s

        x, (kp, ks), (vp, vs), (sp, ss) = _trunk_layer(
            x, l, wb, (norm_mix, norm_mlp), (moba_p, moba_s), (hgrn_p, hgrn_s), (kp, vp),
            a_width=a_width, b_width=b_width)
        outs["sp"].append(sp)
        outs["ks"].append(ks.reshape(db, tn, a_heads, HEAD_DIM))
        outs["vs"].append(vs.reshape(db, tn, a_heads, HEAD_DIM))
        outs["ss"].append(ss)

    y_prompt = _rmsnorm(x[0], final_norm, jnp.float32).reshape(batch, seq, d)
    y_sample = _rmsnorm(x[1], final_norm, jnp.float32).reshape(db, tn, d)
    kv_shape = (depth, batch, seq, a_heads, HEAD_DIM)
    return (y_prompt, y_sample, kp.reshape(kv_shape), vp.reshape(kv_shape),
            jnp.stack(outs["sp"]), jnp.stack(outs["ks"]), jnp.stack(outs["vs"]),
            jnp.stack(outs["ss"]))
```

```python
import functools
import math

import jax
import jax.numpy as jnp
from jax import lax
from jax.experimental import pallas as pl
from jax.experimental.pallas import tpu as pltpu

HEAD_DIM = 128
MOBA_BLOCK = 256
MOBA_TOPK = 3
NUM_BUCKETS = 32
MAX_DISTANCE = 128
PAGE_SIZE = 128
HG_K = 128
HG_V = 128
EPS = 1e-6

V7X_VMEM_LIMIT_BYTES = 60000 * 1024
SUBLANES = 8
LANES = 128

GLA_CHUNK = 256
GLA_GROUP = 4
LOG2_E = math.log2(math.e)
NEG_INF = float("-inf")


def _compiler_params(semantics, vmem_bytes):
    limit = int(min(max(vmem_bytes, 16 * 1024 * 1024), V7X_VMEM_LIMIT_BYTES))
    return pltpu.CompilerParams(dimension_semantics=semantics, vmem_limit_bytes=limit)


def _rmsnorm_body(x_ref, g_ref, o_ref):
    x = x_ref[...]
    ms = jnp.mean(x * x, axis=-1, keepdims=True)
    o_ref[...] = (x * lax.rsqrt(ms + EPS) * g_ref[...]).astype(o_ref.dtype)


def _rmsnorm(x, gain, out_dtype):
    m, d = x.shape
    tm = min(m, 256)
    assert m % tm == 0
    block_bytes = tm * d * (4 + jnp.dtype(out_dtype).itemsize)
    return pl.pallas_call(
        _rmsnorm_body,
        out_shape=jax.ShapeDtypeStruct((m, d), out_dtype),
        grid=(m // tm,),
        in_specs=[pl.BlockSpec((tm, d), lambda i: (i, 0)),
                  pl.BlockSpec((1, d), lambda i: (0, 0))],
        out_specs=pl.BlockSpec((tm, d), lambda i: (i, 0)),
        compiler_params=_compiler_params(("parallel",), 4 * block_bytes),
        name="rmsnorm",
    )(x, gain.reshape(1, d))


def _act_none(a):
    return a


def _act_sigmoid(a):
    return jax.nn.sigmoid(a)


def _act_relu2(a):
    return jnp.square(jnp.maximum(a, 0.0))


def _mm_body(xp_ref, xs_ref, w_ref, *refs, act, has_res, nk):
    op_ref, os_ref = refs[-2:]
    rp_ref, rs_ref = refs[:2] if has_res else (None, None)
    i = pl.program_id(1)
    k = pl.program_id(2)

    def emit(x_ref, r_ref, o_ref):
        if nk == 1:
            acc = jnp.dot(x_ref[...], w_ref[...], preferred_element_type=jnp.float32)
            if has_res:
                acc = acc + r_ref[...]
            o_ref[...] = act(acc).astype(o_ref.dtype)
        else:
            @pl.when(k == 0)
            def _():
                o_ref[...] = r_ref[...] if has_res else jnp.zeros_like(o_ref)

            o_ref[...] += jnp.dot(x_ref[...], w_ref[...], preferred_element_type=jnp.float32)

    emit(xp_ref, rp_ref, op_ref)

    @pl.when(i == 0)
    def _():
        emit(xs_ref, rs_ref, os_ref)


def _matmul(xp, xs, w, *, col0=0, ncols=None, out_dtype, act=_act_none, residual=None,
            stack=None, name):
    m, kdim = xp.shape
    ms = xs.shape[0]
    ncols = w.shape[1] if ncols is None else ncols
    tm = min(m, 1024)
    tk = min(kdim, 4096)
    tn = math.gcd(ncols, col0, 1024)
    assert m % tm == 0 and kdim % tk == 0 and ncols % tn == 0 and col0 % tn == 0
    nm, nn, nk = m // tm, ncols // tn, kdim // tk
    assert nk == 1 or (out_dtype == jnp.float32 and act is _act_none)
    c0 = col0 // tn
    in_specs = [pl.BlockSpec((tm, tk), lambda n, i, k: (i, k)),
                pl.BlockSpec((ms, tk), lambda n, i, k: (0, k)),
                pl.BlockSpec((tk, tn), lambda n, i, k: (k, n + c0))]
    args = [xp, xs, w]
    osize = jnp.dtype(out_dtype).itemsize
    vmem = (2 * ((tm + ms) * tk * 2 + tk * tn * 2 + (tm + ms) * tn * osize)
            + 2 * (tm + ms) * tn * 4)
    if residual is not None:
        in_specs += [pl.BlockSpec((tm, tn), lambda n, i, k: (i, n)),
                     pl.BlockSpec((ms, tn), lambda n, i, k: (0, n))]
        args += list(residual)
        vmem += 2 * (tm + ms) * tn * 4
    aliases = {}
    if stack is None:
        p_shape = jax.ShapeDtypeStruct((m, ncols), out_dtype)
        p_spec = pl.BlockSpec((tm, tn), lambda n, i, k: (i, n))
    else:
        buf, layer = stack
        assert buf.shape[1:] == (m, ncols) and buf.dtype == out_dtype
        p_shape = jax.ShapeDtypeStruct(buf.shape, out_dtype)
        p_spec = pl.BlockSpec((None, tm, tn), lambda n, i, k: (layer, i, n))
        in_specs.append(pl.BlockSpec(memory_space=pl.ANY))
        args.append(buf)
        aliases = {len(args) - 1: 0}
    return pl.pallas_call(
        functools.partial(_mm_body, act=act, has_res=residual is not None, nk=nk),
        out_shape=(p_shape, jax.ShapeDtypeStruct((ms, ncols), out_dtype)),
        grid=(nn, nm, nk),
        in_specs=in_specs,
        out_specs=(p_spec, pl.BlockSpec((ms, tn), lambda n, i, k: (0, n))),
        input_output_aliases=aliases,
        compiler_params=_compiler_params(("arbitrary", "arbitrary", "arbitrary"),
                                         vmem + 4 * 1024 * 1024),
        name=name,
    )(*args)


def _merge_body(oap_ref, obp_ref, oas_ref, obs_ref, wa_ref, wb_ref, gap_ref, gbp_ref, gas_ref,
                gbs_ref, op_ref, os_ref):
    def emit(oa_ref, ob_ref, ga_ref, gb_ref, o_ref):
        a = jnp.dot(oa_ref[...], wa_ref[...], preferred_element_type=jnp.float32)
        b = jnp.dot(ob_ref[...], wb_ref[...], preferred_element_type=jnp.float32)
        o_ref[...] = (ga_ref[...].astype(jnp.float32) * a
                      + gb_ref[...].astype(jnp.float32) * b).astype(o_ref.dtype)

    emit(oap_ref, obp_ref, gap_ref, gbp_ref, op_ref)

    @pl.when(pl.program_id(1) == 0)
    def _():
        emit(oas_ref, obs_ref, gas_ref, gbs_ref, os_ref)


def _merge_branches(o_a, o_b, w_ba, w_bb, gates):
    (oap, oas), (obp, obs), (gp, gs) = o_a, o_b, gates
    m, ka = oap.shape
    ms = oas.shape[0]
    kb = obp.shape[1]
    d = w_ba.shape[1]
    tm = min(m, 1024)
    tn = min(d, 512)
    assert m % tm == 0 and d % tn == 0
    goff = d // tn
    rows = tm + ms
    vmem = 2 * 2 * (rows * ka + rows * kb + ka * tn + kb * tn + 3 * rows * tn) + 3 * rows * tn * 4
    return pl.pallas_call(
        _merge_body,
        out_shape=(jax.ShapeDtypeStruct((m, d), jnp.bfloat16),
                   jax.ShapeDtypeStruct((ms, d), jnp.bfloat16)),
        grid=(d // tn, m // tm),
        in_specs=[pl.BlockSpec((tm, ka), lambda n, i: (i, 0)),
                  pl.BlockSpec((tm, kb), lambda n, i: (i, 0)),
                  pl.BlockSpec((ms, ka), lambda n, i: (0, 0)),
                  pl.BlockSpec((ms, kb), lambda n, i: (0, 0)),
                  pl.BlockSpec((ka, tn), lambda n, i: (0, n)),
                  pl.BlockSpec((kb, tn), lambda n, i: (0, n)),
                  pl.BlockSpec((tm, tn), lambda n, i: (i, n)),
                  pl.BlockSpec((tm, tn), lambda n, i: (i, n + goff)),
                  pl.BlockSpec((ms, tn), lambda n, i: (0, n)),
                  pl.BlockSpec((ms, tn), lambda n, i: (0, n + goff))],
        out_specs=(pl.BlockSpec((tm, tn), lambda n, i: (i, n)),
                   pl.BlockSpec((ms, tn), lambda n, i: (0, n))),
        compiler_params=_compiler_params(("arbitrary", "arbitrary"), vmem + 4 * 1024 * 1024),
        name="merge_branches",
    )(oap, obp, oas, obs, w_ba, w_bb, gp, gp, gs, gs)


def _cast_plan(casts, steps, step_of):
    in_specs, out_specs, out_shapes, args, vmem = [], [], [], [], 0
    for w, layer in casts:
        _, r, c = w.shape
        rows = r // steps
        assert r % steps == 0 and rows % (2 * SUBLANES) == 0
        in_specs.append(pl.BlockSpec((None, rows, c),
                                     lambda *g, layer=layer: (layer, step_of(*g), 0)))
        out_specs.append(pl.BlockSpec((rows, c), lambda *g: (step_of(*g), 0)))
        out_shapes.append(jax.ShapeDtypeStruct((r, c), jnp.bfloat16))
        args.append(w)
        vmem += 2 * rows * c * (4 + 2)
    return in_specs, out_specs, out_shapes, args, vmem


def _emit_casts(src_refs, dst_refs):
    for src, dst in zip(src_refs, dst_refs):
        dst[...] = src[...].astype(dst.dtype)


def _t5_bias(dist, bias_ref, head):
    max_exact = NUM_BUCKETS // 2
    nf = jnp.maximum(dist, 1).astype(jnp.float32)
    far = max_exact + (jnp.log(nf / max_exact) / math.log(MAX_DISTANCE / max_exact)
                       * (NUM_BUCKETS - max_exact)).astype(jnp.int32)
    far = jnp.minimum(far, NUM_BUCKETS - 1)
    bucket = jnp.where(dist < max_exact, dist, far)
    out = jnp.zeros(dist.shape, jnp.float32)
    for k in range(NUM_BUCKETS):
        out = jnp.where(bucket == k, bias_ref[head, k], out)
    return out


def _bias_tiles_body(bias_ref, o_ref):
    h = pl.program_id(0)
    row = lax.broadcasted_iota(jnp.int32, (MOBA_BLOCK, MOBA_BLOCK), 0)
    col = lax.broadcasted_iota(jnp.int32, (MOBA_BLOCK, MOBA_BLOCK), 1)
    own = _t5_bias(jnp.maximum(row - col, 0), bias_ref, h) * LOG2_E
    o_ref[0, 0] = jnp.where(row >= col, own, NEG_INF)
    o_ref[0, 1] = _t5_bias(row - col + MOBA_BLOCK, bias_ref, h) * LOG2_E


def _bias_tiles(bias_t):
    h = bias_t.shape[0]
    return pl.pallas_call(
        _bias_tiles_body,
        out_shape=jax.ShapeDtypeStruct((h, 2, MOBA_BLOCK, MOBA_BLOCK), jnp.float32),
        grid_spec=pltpu.PrefetchScalarGridSpec(
            num_scalar_prefetch=1,
            grid=(h,),
            in_specs=[],
            out_specs=pl.BlockSpec((1, 2, MOBA_BLOCK, MOBA_BLOCK), lambda i, b: (i, 0, 0, 0)),
        ),
        compiler_params=_compiler_params(("arbitrary",), 8 * 1024 * 1024),
        name="t5_bias_tiles",
    )(bias_t)


def _dot_nt(a, b, precision=None):
    return lax.dot_general(a, b, (((1,), (1,)), ((), ())), precision=precision,
                           preferred_element_type=jnp.float32)


def _moba_prompt_body(bias_ref, q_ref, k_ref, v_ref, tiles_ref, *refs, n_blk):
    n_cast = (len(refs) - 1) // 2
    o_ref = refs[n_cast]
    _emit_casts(refs[:n_cast], refs[n_cast + 1:])
    h = pl.program_id(0)
    t = n_blk * MOBA_BLOCK
    n_top = min(MOBA_TOPK, n_blk - 1)
    scale = HEAD_DIM ** -0.5 * LOG2_E
    q = q_ref[...]
    k = k_ref[...]
    qb = (q * scale).astype(jnp.bfloat16)
    kb = k.astype(jnp.bfloat16)
    vb = v_ref[...].astype(jnp.bfloat16)
    own_tile = tiles_ref[0, 0]
    adj_tile = tiles_ref[0, 1]

    if n_top > 0:
        kmean = jnp.mean(k.reshape(n_blk, MOBA_BLOCK, HEAD_DIM), axis=1)
        gate = _dot_nt(kmean, q, precision=lax.Precision.HIGHEST)
        nidx = lax.broadcasted_iota(jnp.int32, (n_blk, t), 0)
        qblk = lax.broadcasted_iota(jnp.int32, (n_blk, t), 1) // MOBA_BLOCK
        rank = jnp.zeros((n_blk, t), jnp.int32)
        for n2 in range(n_blk):
            g2 = gate[n2:n2 + 1, :]
            beats = (n2 < qblk) & ((g2 > gate) | ((g2 == gate) & (n2 < nidx)))
            rank = rank + beats.astype(jnp.int32)
        sel_t = ((nidx < qblk) & (rank < n_top)).astype(jnp.float32)
        pad = jnp.zeros((LANES - n_blk, t), jnp.float32)
        sel = jnp.concatenate([sel_t, pad], axis=0).T
        far_bias = bias_ref[h, NUM_BUCKETS - 1] * LOG2_E

    for j in range(n_blk):
        r0 = j * MOBA_BLOCK
        nk = r0 + MOBA_BLOCK
        if j == 0 or n_top == 0:
            s = _dot_nt(qb[r0:nk], kb[r0:nk]) + own_tile
            vals = vb[r0:nk]
        else:
            s = _dot_nt(qb[r0:nk], kb[:nk])
            parts = []
            for n in range(j):
                near = n == j - 1
                gate_col = jnp.where(sel[r0:nk, n:n + 1] > 0.5, 0.0 if near else far_bias, NEG_INF)
                tile = s[:, n * MOBA_BLOCK:(n + 1) * MOBA_BLOCK] + gate_col
                parts.append(tile + adj_tile if near else tile)
            parts.append(s[:, r0:nk] + own_tile)
            s = jnp.concatenate(parts, axis=1)
            vals = vb[:nk]
        m = jnp.max(s, axis=1, keepdims=True)
        p = jnp.exp2(s - m)
        l = jnp.sum(p, axis=1, keepdims=True)
        o = jnp.dot(p.astype(jnp.bfloat16), vals, preferred_element_type=jnp.float32)
        o_ref[r0:nk, :] = (o / l).astype(o_ref.dtype)


def _moba_prompt(q, k, v, bias_t, tiles, *, layer, batch, seq, casts=()):
    m, width = q.shape
    heads = width // HEAD_DIM
    assert seq % MOBA_BLOCK == 0 and m == batch * seq
    n_blk = seq // MOBA_BLOCK
    spec = pl.BlockSpec((seq, HEAD_DIM), lambda h, b, bias: (b, h))
    kv_spec = pl.BlockSpec((None, seq, HEAD_DIM), lambda h, b, bias: (layer, b, h))
    c_in, c_out, c_shapes, c_args, c_vmem = _cast_plan(
        casts, heads * batch, lambda h, b, bias: h * batch + b)
    vmem = 2 * (3 * seq * HEAD_DIM * 4 + seq * HEAD_DIM * 2) + 8 * MOBA_BLOCK * seq * 4 + c_vmem
    return pl.pallas_call(
        functools.partial(_moba_prompt_body, n_blk=n_blk),
        out_shape=[jax.ShapeDtypeStruct((m, width), jnp.bfloat16)] + c_shapes,
        grid_spec=pltpu.PrefetchScalarGridSpec(
            num_scalar_prefetch=1,
            grid=(heads, batch),
            in_specs=[spec, kv_spec, kv_spec,
                      pl.BlockSpec((1, 2, MOBA_BLOCK, MOBA_BLOCK),
                                   lambda h, b, bias: (h, 0, 0, 0))] + c_in,
            out_specs=[spec] + c_out,
        ),
        compiler_params=_compiler_params(("arbitrary", "arbitrary"), vmem + 8 * 1024 * 1024),
        name="moba_prompt",
    )(bias_t, q, k, v, tiles, *c_args)


PAGES_PER_STEP = 8


def _page_mean_body(pt_ref, *refs, ppb):
    o_ref = refs[-1]
    i = pl.program_id(2)
    blocks = (len(refs) - 1) // ppb
    for j in range(blocks):
        total = sum(jnp.sum(refs[j * ppb + p][0, 0], axis=0) for p in range(ppb))
        o_ref[0, 0, i * blocks + j] = total * (1.0 / MOBA_BLOCK)


def _page_block_means(cache_k, page_table):
    depth, _, page, heads, hd = cache_k.shape
    db, n_pages = page_table.shape
    ppb = MOBA_BLOCK // PAGE_SIZE
    assert page == PAGE_SIZE and n_pages % ppb == 0
    n_full = n_pages // ppb
    pages = math.gcd(n_pages, PAGES_PER_STEP)
    assert pages % ppb == 0

    def page_spec(j):
        return pl.BlockSpec((1, 1, page, heads, hd),
                            lambda l, b, i, pt: (l, pt[b, i * pages + j], 0, 0, 0))

    return pl.pallas_call(
        functools.partial(_page_mean_body, ppb=ppb),
        out_shape=jax.ShapeDtypeStruct((depth, db, n_full, heads, hd), jnp.float32),
        grid_spec=pltpu.PrefetchScalarGridSpec(
            num_scalar_prefetch=1,
            grid=(depth, db, n_pages // pages),
            in_specs=[page_spec(j) for j in range(pages)],
            out_specs=pl.BlockSpec((1, 1, n_full, heads, hd), lambda l, b, i, pt: (l, b, 0, 0, 0)),
        ),
        compiler_params=_compiler_params(("arbitrary",) * 3,
                                         (2 * pages + 4) * page * heads * hd * 4),
        name="page_block_means",
    )(page_table, *([cache_k] * pages))


def _decode_topk_body(q_ref, km_ref, o_ref, *, heads, n_top):
    tn = q_ref.shape[1]
    n_full = km_ref.shape[1]
    lane = lax.broadcasted_iota(jnp.int32, (tn, LANES), 1)
    col = lax.broadcasted_iota(jnp.int32, (tn, n_full), 1).astype(jnp.float32)
    for h in range(heads):
        c0 = h * HEAD_DIM
        gate = _dot_nt(q_ref[0, :, c0:c0 + HEAD_DIM], km_ref[0, :, h, :],
                       precision=lax.Precision.HIGHEST)
        out = jnp.zeros((tn, LANES), jnp.int32)
        for r in range(n_top):
            best = jnp.max(gate, axis=1, keepdims=True)
            idx = jnp.min(jnp.where(gate == best, col, float(n_full)), axis=1, keepdims=True)
            out = jnp.where(lane == r, idx.astype(jnp.int32), out)
            gate = jnp.where(col == idx, NEG_INF, gate)
        o_ref[0, h] = out


def _decode_topk(q, kmeans, *, layer, n_top):
    db, tn, width = q.shape
    _, _, n_full, heads, hd = kmeans.shape
    return pl.pallas_call(
        functools.partial(_decode_topk_body, heads=heads, n_top=n_top),
        out_shape=jax.ShapeDtypeStruct((db, heads, tn, LANES), jnp.int32),
        grid=(db,),
        in_specs=[pl.BlockSpec((1, tn, width), lambda b: (b, 0, 0)),
                  pl.BlockSpec((None, 1, n_full, heads, hd), lambda b: (layer, b, 0, 0, 0))],
        out_specs=pl.BlockSpec((1, heads, tn, LANES), lambda b: (b, 0, 0, 0)),
        compiler_params=_compiler_params(("parallel",), 16 * 1024 * 1024),
        name="decode_topk",
    )(q, kmeans)


def _moba_decode_body(top_ref, pt_ref, bias_ref, q_ref, k_ref, v_ref, ck_ref, cv_ref, o_ref,
                      kbuf, vbuf, sems, *, layer, heads, tn, n_top, past):
    step = pl.program_id(0)
    n_steps = pl.num_programs(0)
    ppb = MOBA_BLOCK // PAGE_SIZE
    n_sel = tn * n_top * MOBA_BLOCK
    own_rows = LANES

    def gather(s, slot, start):
        b = s // heads
        h = s % heads
        for t in range(tn):
            for r in range(n_top):
                blk = top_ref[((b * heads + h) * tn + t) * n_top + r]
                for p in range(ppb):
                    phys = pt_ref[b, blk * ppb + p]
                    row0 = (t * n_top + r) * MOBA_BLOCK + p * PAGE_SIZE
                    for src, dst, sem in ((ck_ref, kbuf, sems.at[slot, 0]),
                                          (cv_ref, vbuf, sems.at[slot, 1])):
                        cp = pltpu.make_async_copy(
                            src.at[layer, phys, :, h, :],
                            dst.at[slot, pl.ds(row0, PAGE_SIZE), :], sem)
                        if start:
                            cp.start()
                        else:
                            cp.wait()

    slot = step % 2

    @pl.when(step == 0)
    def _():
        gather(step, 0, True)

    @pl.when(step + 1 < n_steps)
    def _():
        gather(step + 1, 1 - slot, True)

    gather(step, slot, False)

    b = step // heads
    h = step % heads
    scale = HEAD_DIM ** -0.5
    tp = q_ref.shape[1]
    qb = q_ref[0].astype(jnp.bfloat16)
    zeros = jnp.zeros((own_rows - tp, HEAD_DIM), jnp.float32)
    k_all = jnp.concatenate([kbuf[slot], k_ref[0], zeros], axis=0).astype(jnp.bfloat16)
    v_all = jnp.concatenate([vbuf[slot], v_ref[0], zeros], axis=0).astype(jnp.bfloat16)
    n_cols = n_sel + own_rows
    s = _dot_nt(qb, k_all) * scale

    row = lax.broadcasted_iota(jnp.int32, (tp, n_cols), 0)
    col = lax.broadcasted_iota(jnp.int32, (tp, n_cols), 1)
    seg = col // MOBA_BLOCK
    key_pos = past + (col - n_sel)
    for t in range(tn):
        for r in range(n_top):
            blk = top_ref[((b * heads + h) * tn + t) * n_top + r]
            key_pos = jnp.where(seg == t * n_top + r,
                                blk * MOBA_BLOCK + col % MOBA_BLOCK, key_pos)
    q_pos = past + row
    own = col >= n_sel
    valid = (own & ((col - n_sel) <= row)) | (jnp.logical_not(own) & (seg // n_top == row))
    bias = _t5_bias(jnp.maximum(q_pos - key_pos, 0), bias_ref, h)
    s = jnp.where(valid, s + bias, NEG_INF)
    m = jnp.max(s, axis=1, keepdims=True)
    p = jnp.exp(s - m)
    l = jnp.sum(p, axis=1, keepdims=True)
    o = jnp.dot(p.astype(jnp.bfloat16), v_all, preferred_element_type=jnp.float32)
    o_ref[0] = (o / l).astype(o_ref.dtype)


def _moba_decode(q, k, v, cache_k, cache_v, page_table, top_idx, bias_t, *, layer, heads):
    db, tp, width = q.shape
    tn, n_top = top_idx.shape[-2:]
    past = page_table.shape[1] * PAGE_SIZE
    n_sel = tn * n_top * MOBA_BLOCK
    spec = pl.BlockSpec((1, tp, HEAD_DIM), lambda s, *_: (s // heads, 0, s % heads))
    any_spec = pl.BlockSpec(memory_space=pl.ANY)
    vmem = 2 * 2 * n_sel * HEAD_DIM * 4 + 6 * (n_sel + LANES) * HEAD_DIM * 4
    return pl.pallas_call(
        functools.partial(_moba_decode_body, layer=layer, heads=heads, tn=tn, n_top=n_top,
                          past=past),
        out_shape=jax.ShapeDtypeStruct((db, tp, width), jnp.bfloat16),
        grid_spec=pltpu.PrefetchScalarGridSpec(
            num_scalar_prefetch=3,
            grid=(db * heads,),
            in_specs=[spec, spec, spec, any_spec, any_spec],
            out_specs=spec,
            scratch_shapes=[pltpu.VMEM((2, n_sel, HEAD_DIM), jnp.float32),
                            pltpu.VMEM((2, n_sel, HEAD_DIM), jnp.float32),
                            pltpu.SemaphoreType.DMA((2, 2))],
        ),
        compiler_params=_compiler_params(("arbitrary",), vmem + 8 * 1024 * 1024),
        name="moba_decode",
    )(top_idx.reshape(-1), page_table, bias_t, q, k, v, cache_k, cache_v)


def _hgrn2_gates(qb, fb, layer, lbs):
    q = _silu(qb)
    e = jnp.exp(-jnp.abs(fb))
    t = 1.0 / (1.0 + e)
    et = e * t
    pos = fb >= 0.0
    sig_neg = jnp.where(pos, et, t)
    if layer == 0:
        logf = jnp.minimum(fb, 0.0) - jnp.log(1.0 + e)
        k = sig_neg
    else:
        w = jnp.exp(lbs - jnp.max(lbs, axis=0, keepdims=True))
        p = w / jnp.sum(w, axis=0, keepdims=True)
        lb = jnp.sum(p[:layer + 1], axis=0, keepdims=True) - p[0:1]
        logf = jnp.log(lb + (1.0 - lb) * jnp.where(pos, t, et))
        k = (1.0 - lb) * sig_neg
    return q, logf, k


def _gla_level_ids(c, group):
    t = lax.broadcasted_iota(jnp.int32, (c, c), 0)
    s = lax.broadcasted_iota(jnp.int32, (c, c), 1)
    ids = jnp.full((c, c), -1, jnp.int32)
    size, j = 2 * group, 0
    while size <= c:
        half = size // 2
        hit = (t // size == s // size) & (t % size >= half) & (s % size < half)
        ids = jnp.where(hit, j, ids)
        size, j = 2 * size, j + 1
    return ids


def _gla_chunk(q, k, v, g, st, *, group, tril=None, level_ids=None):
    c, kd = q.shape
    g = g * LOG2_E
    if tril is None:
        row = lax.broadcasted_iota(jnp.int32, q.shape, 0)
        b = g
        sh = 1
        while sh < c:
            b = b + jnp.where(row >= sh, pltpu.roll(b, sh, 0), 0.0)
            sh *= 2
    else:
        hi = g.astype(jnp.bfloat16)
        r1 = g - hi.astype(jnp.float32)
        mid = r1.astype(jnp.bfloat16)
        lo = (r1 - mid.astype(jnp.float32)).astype(jnp.bfloat16)
        b = (jnp.dot(tril, hi, preferred_element_type=jnp.float32)
             + jnp.dot(tril, mid, preferred_element_type=jnp.float32)
             + jnp.dot(tril, lo, preferred_element_type=jnp.float32))

    shape3 = (c // SUBLANES, SUBLANES, kd)
    q3, k3, v3, b3 = (a.reshape(shape3) for a in (q, k, v, b))
    sub = lax.broadcasted_iota(jnp.int32, (1, SUBLANES, kd), 1) % group
    o3 = jnp.sum(q3 * k3, axis=2, keepdims=True) * v3
    for d in range(1, group):
        decay = jnp.exp2(jnp.where(sub >= d, b3 - pltpu.roll(b3, d, 1), NEG_INF))
        w = jnp.sum(q3 * decay * pltpu.roll(k3, d, 1), axis=2, keepdims=True)
        o3 = o3 + w * pltpu.roll(v3, d, 1)
    o = o3.reshape(c, v.shape[1])

    if c > group:
        row = lax.broadcasted_iota(jnp.int32, q.shape, 0)
        scores = jnp.zeros((c, c), jnp.float32)
        size, j = 2 * group, 0
        while size <= c:
            half = size // 2
            bs = b.reshape(c // size, size, kd)
            ref = jnp.broadcast_to(bs[:, half - 1:half, :], bs.shape).reshape(b.shape)
            upper = (row % size) >= half
            diff = b - ref
            x = (jnp.where(upper, q, k) * jnp.exp2(jnp.where(upper, diff, -diff))
                 ).astype(jnp.bfloat16)
            scores = jnp.where(level_ids == j, _dot_nt(x, x), scores)
            size, j = 2 * size, j + 1
        o = o + jnp.dot(scores.astype(jnp.bfloat16), v.astype(jnp.bfloat16),
                        preferred_element_type=jnp.float32)

    b_last = b[c - 1:c, :]
    o = o + _dot_nt((q * jnp.exp2(b)).astype(jnp.bfloat16), st.astype(jnp.bfloat16))
    kw = (k * jnp.exp2(b_last - b)).astype(jnp.bfloat16)
    st_new = st * jnp.exp2(b_last) + lax.dot_general(
        v.astype(jnp.bfloat16), kw, (((0,), (0,)), ((), ())), preferred_element_type=jnp.float32)
    return o, st_new


def _silu(x):
    return x / (1.0 + jnp.exp(-x))


def _hgrn2_out(o, gate, gain):
    ms = jnp.mean(o * o, axis=-1, keepdims=True)
    return o * lax.rsqrt(ms + EPS) * gain * _silu(gate)


def _hgrn2_prompt_body(qb_ref, fb_ref, ib_ref, gb_ref, lb_ref, gain_ref, *refs, layer, n_chunks):
    n_cast = (len(refs) - 3) // 2
    o_ref, s_ref = refs[n_cast:n_cast + 2]
    st_ref = refs[-1]
    _emit_casts(refs[:n_cast], refs[n_cast + 2:-1])
    st_ref[...] = jnp.zeros_like(st_ref)
    lbs = lb_ref[...]
    gain = gain_ref[layer:layer + 1, :]
    tril = (lax.broadcasted_iota(jnp.int32, (GLA_CHUNK, GLA_CHUNK), 0)
            >= lax.broadcasted_iota(jnp.int32, (GLA_CHUNK, GLA_CHUNK), 1)).astype(jnp.bfloat16)
    level_ids = _gla_level_ids(GLA_CHUNK, GLA_GROUP)

    def chunk(i, carry):
        rows = pl.ds(pl.multiple_of(i * GLA_CHUNK, GLA_CHUNK), GLA_CHUNK)
        q, logf, k = _hgrn2_gates(qb_ref[rows, :], fb_ref[rows, :], layer, lbs)
        o, st = _gla_chunk(q, k, ib_ref[rows, :], logf, st_ref[...], group=GLA_GROUP, tril=tril,
                           level_ids=level_ids)
        st_ref[...] = st
        o_ref[rows, :] = _hgrn2_out(o, gb_ref[rows, :], gain).astype(o_ref.dtype)
        return carry

    lax.fori_loop(0, n_chunks, chunk, 0, unroll=4 if n_chunks % 4 == 0 else 1)
    s_ref[0, 0] = st_ref[...].T


def _hgrn2_prompt(proj, lower_logits, gain, *, layer, batch, seq, heads, casts=()):
    m = proj.shape[0]
    assert seq % GLA_CHUNK == 0 and m == batch * seq
    depth = lower_logits.shape[0]

    def col(j):
        return pl.BlockSpec((seq, HG_K), lambda b, h: (b, j * heads + h))

    par = pl.BlockSpec((depth, HG_K), lambda b, h: (0, h))
    gspec = pl.BlockSpec((depth, HG_V), lambda b, h: (0, 0))
    c_in, c_out, c_shapes, c_args, c_vmem = _cast_plan(
        casts, batch * heads, lambda b, h: b * heads + h)
    vmem = (2 * (4 * seq * HG_K * 4 + seq * HG_V * 2) + 16 * GLA_CHUNK * GLA_CHUNK * 4
            + c_vmem)
    return pl.pallas_call(
        functools.partial(_hgrn2_prompt_body, layer=layer, n_chunks=seq // GLA_CHUNK),
        out_shape=[jax.ShapeDtypeStruct((m, heads * HG_V), jnp.bfloat16),
                   jax.ShapeDtypeStruct((batch, heads, HG_K, HG_V), jnp.float32)] + c_shapes,
        grid=(batch, heads),
        in_specs=[col(0), col(1), col(2), col(3), par, gspec] + c_in,
        out_specs=[pl.BlockSpec((seq, HG_V), lambda b, h: (b, h)),
                   pl.BlockSpec((1, 1, HG_K, HG_V), lambda b, h: (b, h, 0, 0))] + c_out,
        scratch_shapes=[pltpu.VMEM((HG_V, HG_K), jnp.float32)],
        compiler_params=_compiler_params(("arbitrary", "arbitrary"), vmem + 8 * 1024 * 1024),
        name="hgrn2_prompt",
    )(proj, proj, proj, proj, lower_logits, gain, *c_args)


def _hgrn2_decode_body(qb_ref, fb_ref, ib_ref, gb_ref, s0_ref, lb_ref, gain_ref, o_ref, s_ref,
                       *, layer, tn):
    keep = lax.broadcasted_iota(jnp.int32, (SUBLANES, HG_K), 0) < tn
    q, logf, k = _hgrn2_gates(qb_ref[0], fb_ref[0], layer, lb_ref[...])
    logf = jnp.where(keep, logf, 0.0)
    k = jnp.where(keep, k, 0.0)
    o, st = _gla_chunk(q, k, ib_ref[0], logf, s0_ref[0, 0].T, group=SUBLANES)
    s_ref[0, 0] = st.T
    o_ref[0] = _hgrn2_out(o, gb_ref[0], gain_ref[layer:layer + 1, :]).astype(o_ref.dtype)


def _hgrn2_decode(proj, s0, lower_logits, gain, *, layer, heads, tn):
    db, tp, _ = proj.shape
    assert tn <= tp == SUBLANES
    depth = lower_logits.shape[0]

    def col(j):
        return pl.BlockSpec((1, tp, HG_K), lambda b, h: (b, 0, j * heads + h))

    sspec = pl.BlockSpec((1, 1, HG_K, HG_V), lambda b, h: (b, h, 0, 0))
    return pl.pallas_call(
        functools.partial(_hgrn2_decode_body, layer=layer, tn=tn),
        out_shape=(jax.ShapeDtypeStruct((db, tp, heads * HG_V), jnp.bfloat16),
                   jax.ShapeDtypeStruct(s0.shape, jnp.float32)),
        grid=(db, heads),
        in_specs=[col(0), col(1), col(2), col(3), sspec,
                  pl.BlockSpec((depth, HG_K), lambda b, h: (0, h)),
                  pl.BlockSpec((depth, HG_V), lambda b, h: (0, 0))],
        out_specs=(pl.BlockSpec((1, tp, HG_V), lambda b, h: (b, 0, h)), sspec),
        compiler_params=_compiler_params(("parallel", "parallel"), 16 * 1024 * 1024),
        name="hgrn2_decode",
    )(proj, proj, proj, proj, s0, lower_logits, gain)


def _trunk_layer(x, layer, wb, norms, moba, hgrn2, kv_stack, *, a_width, b_width):
    g_mix, g_mlp = norms
    d = x[0].shape[1]
    f32, bf16 = jnp.float32, jnp.bfloat16
    h = [_rmsnorm(xi, g_mix[layer], bf16) for xi in x]
    proj = functools.partial(_matmul, h[0], h[1], wb["in", layer])
    qa = proj(col0=0, ncols=a_width, out_dtype=f32, name="proj_q")
    ka = proj(col0=a_width, ncols=a_width, out_dtype=f32, stack=(kv_stack[0], layer),
              name="proj_k")
    va = proj(col0=2 * a_width, ncols=a_width, out_dtype=f32, stack=(kv_stack[1], layer),
              name="proj_v")
    hg = proj(col0=3 * a_width, ncols=4 * b_width, out_dtype=f32, name="proj_hgrn")
    gates = proj(col0=3 * a_width + 4 * b_width, ncols=2 * d, out_dtype=bf16, act=_act_sigmoid,
                 name="proj_gates")
    o_a = [f(q, k, v) for f, q, k, v in zip(moba, qa, ka, va)]
    o_b, s_new = zip(*[f(g) for f, g in zip(hgrn2, hg)])
    merged = _merge_branches(o_a, o_b, wb["ba", layer], wb["bb", layer], gates)
    x = _matmul(*merged, wb["out", layer], out_dtype=f32, residual=x, name="out_proj")
    h2 = [_rmsnorm(xi, g_mlp[layer], bf16) for xi in x]
    u = _matmul(*h2, wb["up", layer], out_dtype=bf16, act=_act_relu2, name="ffn_up")
    x = _matmul(*u, wb["down", layer], out_dtype=f32, residual=x, name="ffn_down")
    return x, ka, va, s_new


def kernel(x_prompt, x_sample, cache_k, cache_v, state_hgrn, page_table, w_in, w_branch_a,
           w_branch_b, w_out, w_up, w_down, norm_mix, norm_mlp, hg_norm, hg_lower_bounds,
           rel_bias, final_norm):
    batch, seq, d = x_prompt.shape
    db, tn, _ = x_sample.shape
    depth = w_in.shape[0]
    a_width = w_branch_a.shape[1]
    b_width = w_branch_b.shape[1]
    a_heads = a_width // HEAD_DIM
    b_heads = b_width // HG_V
    n_pages = page_table.shape[1]
    n_full = (n_pages * PAGE_SIZE) // MOBA_BLOCK
    assert (n_pages * PAGE_SIZE) % MOBA_BLOCK == 0
    n_top = min(MOBA_TOPK, n_full)
    assert n_top > 0 and tn <= SUBLANES

    bf16 = jnp.bfloat16
    bias_t = rel_bias.astype(jnp.float32).T
    tiles = _bias_tiles(bias_t)
    kmeans = _page_block_means(cache_k, page_table)

    x = (x_prompt.reshape(batch * seq, d), x_sample.reshape(db * tn, d))
    w_f32 = {"in": w_in, "ba": w_branch_a, "bb": w_branch_b, "out": w_out, "up": w_up,
             "down": w_down}
    wb = {("in", 0): w_in[0].astype(bf16)}
    outs = {name: [] for name in ("sp", "ks", "vs", "ss")}
    kp = jnp.zeros((depth, batch * seq, a_width), jnp.float32)
    vp = jnp.zeros((depth, batch * seq, a_width), jnp.float32)
    for l in range(depth):
        nxt = [l + 1] if l + 1 < depth else []

        def with_casts(fn, n_out, keys):
            def run(*args):
                res = fn(*args, casts=[(w_f32[name], lyr) for name, lyr in keys])
                wb.update(zip(keys, res[n_out:]))
                return res[0] if n_out == 1 else tuple(res[:n_out])
            return run

        moba_p = with_casts(
            functools.partial(_moba_prompt, bias_t=bias_t, tiles=tiles, layer=l, batch=batch,
                              seq=seq),
            1, [(name, l) for name in ("up", "out", "ba", "bb")])
        hgrn_p = with_casts(
            functools.partial(_hgrn2_prompt, lower_logits=hg_lower_bounds, gain=hg_norm, layer=l,
                              batch=batch, seq=seq, heads=b_heads),
            2, [("down", l)] + [("in", j) for j in nxt])

        def pad_tokens(a):
            return jnp.pad(a.reshape(db, tn, -1), ((0, 0), (0, SUBLANES - tn), (0, 0)))

        def moba_s(q, k, v, l=l):
            q3, k3, v3 = pad_tokens(q), pad_tokens(k), pad_tokens(v)
            top = _decode_topk(q3, kmeans, layer=l, n_top=n_top)[:, :, :tn, :n_top]
            o = _moba_decode(q3, k3, v3, cache_k, cache_v, page_table, top, bias_t, layer=l,
                             heads=a_heads)
            return o[:, :tn].reshape(db * tn, a_width)

        def hgrn_s(hg, l=l):
            o, s = _hgrn2_decode(pad_tokens(hg), state_hgrn[l], hg_lower_bounds, hg_norm,
                                 layer=l, heads=b_heads, tn=tn)
            return o[:, :tn].reshape(db * tn, b_width), s

        x, (kp, ks), (vp, vs), (sp, ss) = _trunk_layer(
            x, l, wb, (norm_mix, norm_mlp), (moba_p, moba_s), (hgrn_p, hgrn_s), (kp, vp),
            a_width=a_width, b_width=b_width)
        outs["sp"].append(sp)
        outs["ks"].append(ks.reshape(db, tn, a_heads, HEAD_DIM))
        outs["vs"].append(vs.reshape(db, tn, a_heads, HEAD_DIM))
        outs["ss"].append(ss)

    y_prompt = _rmsnorm(x[0], final_norm, jnp.float32).reshape(batch, seq, d)
    y_sample = _rmsnorm(x[1], final_norm, jnp.float32).reshape(db, tn, d)
    kv_shape = (depth, batch, seq, a_heads, HEAD_DIM)
    return (y_prompt, y_sample, kp.reshape(kv_shape), vp.reshape(kv_shape),
            jnp.stack(outs["sp"]), jnp.stack(outs["ks"]), jnp.stack(outs["vs"]),
            jnp.stack(outs["ss"]))
```

```python
import functools
import math

import jax
import jax.numpy as jnp
from jax import lax
from jax.experimental import pallas as pl
from jax.experimental.pallas import tpu as pltpu

HEAD_DIM = 128
MOBA_BLOCK = 256
MOBA_TOPK = 3
NUM_BUCKETS = 32
MAX_DISTANCE = 128
PAGE_SIZE = 128
HG_K = 128
HG_V = 128
EPS = 1e-6

V7X_VMEM_LIMIT_BYTES = 60000 * 1024
SUBLANES = 8
LANES = 128

GLA_CHUNK = 256
GLA_GROUP = 4
LOG2_E = math.log2(math.e)
NEG_INF = float("-inf")


def _compiler_params(semantics, vmem_bytes):
    limit = int(min(max(vmem_bytes, 16 * 1024 * 1024), V7X_VMEM_LIMIT_BYTES))
    return pltpu.CompilerParams(dimension_semantics=semantics, vmem_limit_bytes=limit)


def _rmsnorm_body(x_ref, g_ref, o_ref):
    x = x_ref[...]
    ms = jnp.mean(x * x, axis=-1, keepdims=True)
    o_ref[...] = (x * lax.rsqrt(ms + EPS) * g_ref[...]).astype(o_ref.dtype)


def _rmsnorm(x, gain, out_dtype):
    m, d = x.shape
    tm = min(m, 512)
    assert m % tm == 0
    block_bytes = tm * d * (4 + jnp.dtype(out_dtype).itemsize)
    return pl.pallas_call(
        _rmsnorm_body,
        out_shape=jax.ShapeDtypeStruct((m, d), out_dtype),
        grid=(m // tm,),
        in_specs=[pl.BlockSpec((tm, d), lambda i: (i, 0)),
                  pl.BlockSpec((1, d), lambda i: (0, 0))],
        out_specs=pl.BlockSpec((tm, d), lambda i: (i, 0)),
        compiler_params=_compiler_params(("parallel",), 4 * block_bytes),
        name="rmsnorm",
    )(x, gain.reshape(1, d))


def _act_none(a):
    return a


def _act_sigmoid(a):
    return jax.nn.sigmoid(a)


def _act_relu2(a):
    return jnp.square(jnp.maximum(a, 0.0))


def _mm_body(xp_ref, xs_ref, w_ref, *refs, act, has_res, nk):
    op_ref, os_ref = refs[-2:]
    rp_ref, rs_ref = refs[:2] if has_res else (None, None)
    i = pl.program_id(1)
    k = pl.program_id(2)

    def emit(x_ref, r_ref, o_ref):
        if nk == 1:
            acc = jnp.dot(x_ref[...], w_ref[...], preferred_element_type=jnp.float32)
            if has_res:
                acc = acc + r_ref[...]
            o_ref[...] = act(acc).astype(o_ref.dtype)
        else:
            @pl.when(k == 0)
            def _():
                o_ref[...] = r_ref[...] if has_res else jnp.zeros_like(o_ref)

            o_ref[...] += jnp.dot(x_ref[...], w_ref[...], preferred_element_type=jnp.float32)

    emit(xp_ref, rp_ref, op_ref)

    @pl.when(i == 0)
    def _():
        emit(xs_ref, rs_ref, os_ref)


def _matmul(xp, xs, w, *, col0=0, ncols=None, out_dtype, act=_act_none, residual=None,
            stack=None, name):
    m, kdim = xp.shape
    ms = xs.shape[0]
    ncols = w.shape[1] if ncols is None else ncols
    tm = min(m, 1024)
    tk = min(kdim, 4096)
    tn = math.gcd(ncols, col0, 1024)
    assert m % tm == 0 and kdim % tk == 0 and ncols % tn == 0 and col0 % tn == 0
    nm, nn, nk = m // tm, ncols // tn, kdim // tk
    assert nk == 1 or (out_dtype == jnp.float32 and act is _act_none)
    c0 = col0 // tn
    in_specs = [pl.BlockSpec((tm, tk), lambda n, i, k: (i, k)),
                pl.BlockSpec((ms, tk), lambda n, i, k: (0, k)),
                pl.BlockSpec((tk, tn), lambda n, i, k: (k, n + c0))]
    args = [xp, xs, w]
    osize = jnp.dtype(out_dtype).itemsize
    vmem = (2 * ((tm + ms) * tk * 2 + tk * tn * 2 + (tm + ms) * tn * osize)
            + 2 * (tm + ms) * tn * 4)
    if residual is not None:
        in_specs += [pl.BlockSpec((tm, tn), lambda n, i, k: (i, n)),
                     pl.BlockSpec((ms, tn), lambda n, i, k: (0, n))]
        args += list(residual)
        vmem += 2 * (tm + ms) * tn * 4
    aliases = {}
    if stack is None:
        p_shape = jax.ShapeDtypeStruct((m, ncols), out_dtype)
        p_spec = pl.BlockSpec((tm, tn), lambda n, i, k: (i, n))
    else:
        buf, layer = stack
        assert buf.shape[1:] == (m, ncols) and buf.dtype == out_dtype
        p_shape = jax.ShapeDtypeStruct(buf.shape, out_dtype)
        p_spec = pl.BlockSpec((None, tm, tn), lambda n, i, k: (layer, i, n))
        in_specs.append(pl.BlockSpec(memory_space=pl.ANY))
        args.append(buf)
        aliases = {len(args) - 1: 0}
    return pl.pallas_call(
        functools.partial(_mm_body, act=act, has_res=residual is not None, nk=nk),
        out_shape=(p_shape, jax.ShapeDtypeStruct((ms, ncols), out_dtype)),
        grid=(nn, nm, nk),
        in_specs=in_specs,
        out_specs=(p_spec, pl.BlockSpec((ms, tn), lambda n, i, k: (0, n))),
        input_output_aliases=aliases,
        compiler_params=_compiler_params(("arbitrary", "arbitrary", "arbitrary"),
                                         vmem + 4 * 1024 * 1024),
        name=name,
    )(*args)


def _merge_body(oap_ref, obp_ref, oas_ref, obs_ref, wa_ref, wb_ref, gap_ref, gbp_ref, gas_ref,
                gbs_ref, op_ref, os_ref):
    def emit(oa_ref, ob_ref, ga_ref, gb_ref, o_ref):
        a = jnp.dot(oa_ref[...], wa_ref[...], preferred_element_type=jnp.float32)
        b = jnp.dot(ob_ref[...], wb_ref[...], preferred_element_type=jnp.float32)
        o_ref[...] = (ga_ref[...].astype(jnp.float32) * a
                      + gb_ref[...].astype(jnp.float32) * b).astype(o_ref.dtype)

    emit(oap_ref, obp_ref, gap_ref, gbp_ref, op_ref)

    @pl.when(pl.program_id(1) == 0)
    def _():
        emit(oas_ref, obs_ref, gas_ref, gbs_ref, os_ref)


def _merge_branches(o_a, o_b, w_ba, w_bb, gates):
    (oap, oas), (obp, obs), (gp, gs) = o_a, o_b, gates
    m, ka = oap.shape
    ms = oas.shape[0]
    kb = obp.shape[1]
    d = w_ba.shape[1]
    tm = min(m, 1024)
    tn = min(d, 512)
    assert m % tm == 0 and d % tn == 0
    goff = d // tn
    rows = tm + ms
    vmem = 2 * 2 * (rows * ka + rows * kb + ka * tn + kb * tn + 3 * rows * tn) + 3 * rows * tn * 4
    return pl.pallas_call(
        _merge_body,
        out_shape=(jax.ShapeDtypeStruct((m, d), jnp.bfloat16),
                   jax.ShapeDtypeStruct((ms, d), jnp.bfloat16)),
        grid=(d // tn, m // tm),
        in_specs=[pl.BlockSpec((tm, ka), lambda n, i: (i, 0)),
                  pl.BlockSpec((tm, kb), lambda n, i: (i, 0)),
                  pl.BlockSpec((ms, ka), lambda n, i: (0, 0)),
                  pl.BlockSpec((ms, kb), lambda n, i: (0, 0)),
                  pl.BlockSpec((ka, tn), lambda n, i: (0, n)),
                  pl.BlockSpec((kb, tn), lambda n, i: (0, n)),
                  pl.BlockSpec((tm, tn), lambda n, i: (i, n)),
                  pl.BlockSpec((tm, tn), lambda n, i: (i, n + goff)),
                  pl.BlockSpec((ms, tn), lambda n, i: (0, n)),
                  pl.BlockSpec((ms, tn), lambda n, i: (0, n + goff))],
        out_specs=(pl.BlockSpec((tm, tn), lambda n, i: (i, n)),
                   pl.BlockSpec((ms, tn), lambda n, i: (0, n))),
        compiler_params=_compiler_params(("arbitrary", "arbitrary"), vmem + 4 * 1024 * 1024),
        name="merge_branches",
    )(oap, obp, oas, obs, w_ba, w_bb, gp, gp, gs, gs)


def _cast_plan(casts, steps, step_of):
    in_specs, out_specs, out_shapes, args, vmem = [], [], [], [], 0
    for w, layer in casts:
        _, r, c = w.shape
        rows = r // steps
        assert r % steps == 0 and rows % (2 * SUBLANES) == 0
        in_specs.append(pl.BlockSpec((None, rows, c),
                                     lambda *g, layer=layer: (layer, step_of(*g), 0)))
        out_specs.append(pl.BlockSpec((rows, c), lambda *g: (step_of(*g), 0)))
        out_shapes.append(jax.ShapeDtypeStruct((r, c), jnp.bfloat16))
        args.append(w)
        vmem += 2 * rows * c * (4 + 2)
    return in_specs, out_specs, out_shapes, args, vmem


def _emit_casts(src_refs, dst_refs):
    for src, dst in zip(src_refs, dst_refs):
        dst[...] = src[...].astype(dst.dtype)


def _t5_bias(dist, bias_ref, head):
    max_exact = NUM_BUCKETS // 2
    nf = jnp.maximum(dist, 1).astype(jnp.float32)
    far = max_exact + (jnp.log(nf / max_exact) / math.log(MAX_DISTANCE / max_exact)
                       * (NUM_BUCKETS - max_exact)).astype(jnp.int32)
    far = jnp.minimum(far, NUM_BUCKETS - 1)
    bucket = jnp.where(dist < max_exact, dist, far)
    out = jnp.zeros(dist.shape, jnp.float32)
    for k in range(NUM_BUCKETS):
        out = jnp.where(bucket == k, bias_ref[head, k], out)
    return out


def _bias_tiles_body(bias_ref, o_ref):
    h = pl.program_id(0)
    row = lax.broadcasted_iota(jnp.int32, (MOBA_BLOCK, MOBA_BLOCK), 0)
    col = lax.broadcasted_iota(jnp.int32, (MOBA_BLOCK, MOBA_BLOCK), 1)
    own = _t5_bias(jnp.maximum(row - col, 0), bias_ref, h) * LOG2_E
    o_ref[0, 0] = jnp.where(row >= col, own, NEG_INF)
    o_ref[0, 1] = _t5_bias(row - col + MOBA_BLOCK, bias_ref, h) * LOG2_E


def _bias_tiles(bias_t):
    h = bias_t.shape[0]
    return pl.pallas_call(
        _bias_tiles_body,
        out_shape=jax.ShapeDtypeStruct((h, 2, MOBA_BLOCK, MOBA_BLOCK), jnp.float32),
        grid_spec=pltpu.PrefetchScalarGridSpec(
            num_scalar_prefetch=1,
            grid=(h,),
            in_specs=[],
            out_specs=pl.BlockSpec((1, 2, MOBA_BLOCK, MOBA_BLOCK), lambda i, b: (i, 0, 0, 0)),
        ),
        compiler_params=_compiler_params(("arbitrary",), 8 * 1024 * 1024),
        name="t5_bias_tiles",
    )(bias_t)


def _dot_nt(a, b, precision=None):
    return lax.dot_general(a, b, (((1,), (1,)), ((), ())), precision=precision,
                           preferred_element_type=jnp.float32)


def _moba_prompt_body(bias_ref, q_ref, k_ref, v_ref, tiles_ref, *refs, n_blk):
    n_cast = (len(refs) - 1) // 2
    o_ref = refs[n_cast]
    _emit_casts(refs[:n_cast], refs[n_cast + 1:])
    h = pl.program_id(0)
    t = n_blk * MOBA_BLOCK
    n_top = min(MOBA_TOPK, n_blk - 1)
    scale = HEAD_DIM ** -0.5 * LOG2_E
    q = q_ref[...]
    k = k_ref[...]
    qb = (q * scale).astype(jnp.bfloat16)
    kb = k.astype(jnp.bfloat16)
    vb = v_ref[...].astype(jnp.bfloat16)
    own_tile = tiles_ref[0, 0]
    adj_tile = tiles_ref[0, 1]

    if n_top > 0:
        kmean = jnp.mean(k.reshape(n_blk, MOBA_BLOCK, HEAD_DIM), axis=1)
        gate = _dot_nt(kmean, q, precision=lax.Precision.HIGHEST)
        nidx = lax.broadcasted_iota(jnp.int32, (n_blk, t), 0)
        qblk = lax.broadcasted_iota(jnp.int32, (n_blk, t), 1) // MOBA_BLOCK
        rank = jnp.zeros((n_blk, t), jnp.int32)
        for n2 in range(n_blk):
            g2 = gate[n2:n2 + 1, :]
            beats = (n2 < qblk) & ((g2 > gate) | ((g2 == gate) & (n2 < nidx)))
            rank = rank + beats.astype(jnp.int32)
        sel_t = ((nidx < qblk) & (rank < n_top)).astype(jnp.float32)
        pad = jnp.zeros((LANES - n_blk, t), jnp.float32)
        sel = jnp.concatenate([sel_t, pad], axis=0).T
        far_bias = bias_ref[h, NUM_BUCKETS - 1] * LOG2_E

    for j in range(n_blk):
        r0 = j * MOBA_BLOCK
        nk = r0 + MOBA_BLOCK
        if j == 0 or n_top == 0:
            s = _dot_nt(qb[r0:nk], kb[r0:nk]) + own_tile
            vals = vb[r0:nk]
        else:
            s = _dot_nt(qb[r0:nk], kb[:nk])
            parts = []
            for n in range(j):
                near = n == j - 1
                gate_col = jnp.where(sel[r0:nk, n:n + 1] > 0.5, 0.0 if near else far_bias, NEG_INF)
                tile = s[:, n * MOBA_BLOCK:(n + 1) * MOBA_BLOCK] + gate_col
                parts.append(tile + adj_tile if near else tile)
            parts.append(s[:, r0:nk] + own_tile)
            s = jnp.concatenate(parts, axis=1)
            vals = vb[:nk]
        m = jnp.max(s, axis=1, keepdims=True)
        p = jnp.exp2(s - m)
        l = jnp.sum(p, axis=1, keepdims=True)
        o = jnp.dot(p.astype(jnp.bfloat16), vals, preferred_element_type=jnp.float32)
        o_ref[r0:nk, :] = (o / l).astype(o_ref.dtype)


def _moba_prompt(q, k, v, bias_t, tiles, *, layer, batch, seq, casts=()):
    m, width = q.shape
    heads = width // HEAD_DIM
    assert seq % MOBA_BLOCK == 0 and m == batch * seq
    n_blk = seq // MOBA_BLOCK
    spec = pl.BlockSpec((seq, HEAD_DIM), lambda h, b, bias: (b, h))
    kv_spec = pl.BlockSpec((None, seq, HEAD_DIM), lambda h, b, bias: (layer, b, h))
    c_in, c_out, c_shapes, c_args, c_vmem = _cast_plan(
        casts, heads * batch, lambda h, b, bias: h * batch + b)
    vmem = 2 * (3 * seq * HEAD_DIM * 4 + seq * HEAD_DIM * 2) + 8 * MOBA_BLOCK * seq * 4 + c_vmem
    return pl.pallas_call(
        functools.partial(_moba_prompt_body, n_blk=n_blk),
        out_shape=[jax.ShapeDtypeStruct((m, width), jnp.bfloat16)] + c_shapes,
        grid_spec=pltpu.PrefetchScalarGridSpec(
            num_scalar_prefetch=1,
            grid=(heads, batch),
            in_specs=[spec, kv_spec, kv_spec,
                      pl.BlockSpec((1, 2, MOBA_BLOCK, MOBA_BLOCK),
                                   lambda h, b, bias: (h, 0, 0, 0))] + c_in,
            out_specs=[spec] + c_out,
        ),
        compiler_params=_compiler_params(("arbitrary", "arbitrary"), vmem + 8 * 1024 * 1024),
        name="moba_prompt",
    )(bias_t, q, k, v, tiles, *c_args)


PAGES_PER_STEP = 8


def _page_mean_body(pt_ref, *refs, ppb):
    o_ref = refs[-1]
    i = pl.program_id(2)
    blocks = (len(refs) - 1) // ppb
    for j in range(blocks):
        total = sum(jnp.sum(refs[j * ppb + p][0, 0], axis=0) for p in range(ppb))
        o_ref[0, 0, i * blocks + j] = total * (1.0 / MOBA_BLOCK)


def _page_block_means(cache_k, page_table):
    depth, _, page, heads, hd = cache_k.shape
    db, n_pages = page_table.shape
    ppb = MOBA_BLOCK // PAGE_SIZE
    assert page == PAGE_SIZE and n_pages % ppb == 0
    n_full = n_pages // ppb
    pages = math.gcd(n_pages, PAGES_PER_STEP)
    assert pages % ppb == 0

    def page_spec(j):
        return pl.BlockSpec((1, 1, page, heads, hd),
                            lambda l, b, i, pt: (l, pt[b, i * pages + j], 0, 0, 0))

    return pl.pallas_call(
        functools.partial(_page_mean_body, ppb=ppb),
        out_shape=jax.ShapeDtypeStruct((depth, db, n_full, heads, hd), jnp.float32),
        grid_spec=pltpu.PrefetchScalarGridSpec(
            num_scalar_prefetch=1,
            grid=(depth, db, n_pages // pages),
            in_specs=[page_spec(j) for j in range(pages)],
            out_specs=pl.BlockSpec((1, 1, n_full, heads, hd), lambda l, b, i, pt: (l, b, 0, 0, 0)),
        ),
        compiler_params=_compiler_params(("arbitrary",) * 3,
                                         (2 * pages + 4) * page * heads * hd * 4),
        name="page_block_means",
    )(page_table, *([cache_k] * pages))


def _decode_topk_body(q_ref, km_ref, o_ref, *, heads, n_top):
    tn = q_ref.shape[1]
    n_full = km_ref.shape[1]
    lane = lax.broadcasted_iota(jnp.int32, (tn, LANES), 1)
    col = lax.broadcasted_iota(jnp.int32, (tn, n_full), 1).astype(jnp.float32)
    for h in range(heads):
        c0 = h * HEAD_DIM
        gate = _dot_nt(q_ref[0, :, c0:c0 + HEAD_DIM], km_ref[0, :, h, :],
                       precision=lax.Precision.HIGHEST)
        out = jnp.zeros((tn, LANES), jnp.int32)
        for r in range(n_top):
            best = jnp.max(gate, axis=1, keepdims=True)
            idx = jnp.min(jnp.where(gate == best, col, float(n_full)), axis=1, keepdims=True)
            out = jnp.where(lane == r, idx.astype(jnp.int32), out)
            gate = jnp.where(col == idx, NEG_INF, gate)
        o_ref[0, h] = out


def _decode_topk(q, kmeans, *, layer, n_top):
    db, tn, width = q.shape
    _, _, n_full, heads, hd = kmeans.shape
    return pl.pallas_call(
        functools.partial(_decode_topk_body, heads=heads, n_top=n_top),
        out_shape=jax.ShapeDtypeStruct((db, heads, tn, LANES), jnp.int32),
        grid=(db,),
        in_specs=[pl.BlockSpec((1, tn, width), lambda b: (b, 0, 0)),
                  pl.BlockSpec((None, 1, n_full, heads, hd), lambda b: (layer, b, 0, 0, 0))],
        out_specs=pl.BlockSpec((1, heads, tn, LANES), lambda b: (b, 0, 0, 0)),
        compiler_params=_compiler_params(("parallel",), 16 * 1024 * 1024),
        name="decode_topk",
    )(q, kmeans)


def _moba_decode_body(top_ref, pt_ref, bias_ref, q_ref, k_ref, v_ref, ck_ref, cv_ref, o_ref,
                      kbuf, vbuf, sems, *, layer, heads, tn, n_top, past):
    step = pl.program_id(0)
    n_steps = pl.num_programs(0)
    ppb = MOBA_BLOCK // PAGE_SIZE
    n_sel = tn * n_top * MOBA_BLOCK
    own_rows = LANES

    def gather(s, slot, start):
        b = s // heads
        h = s % heads
        for t in range(tn):
            for r in range(n_top):
                blk = top_ref[((b * heads + h) * tn + t) * n_top + r]
                for p in range(ppb):
                    phys = pt_ref[b, blk * ppb + p]
                    row0 = (t * n_top + r) * MOBA_BLOCK + p * PAGE_SIZE
                    for src, dst, sem in ((ck_ref, kbuf, sems.at[slot, 0]),
                                          (cv_ref, vbuf, sems.at[slot, 1])):
                        cp = pltpu.make_async_copy(
                            src.at[layer, phys, :, h, :],
                            dst.at[slot, pl.ds(row0, PAGE_SIZE), :], sem)
                        if start:
                            cp.start()
                        else:
                            cp.wait()

    slot = step % 2

    @pl.when(step == 0)
    def _():
        gather(step, 0, True)

    @pl.when(step + 1 < n_steps)
    def _():
        gather(step + 1, 1 - slot, True)

    gather(step, slot, False)

    b = step // heads
    h = step % heads
    scale = HEAD_DIM ** -0.5
    tp = q_ref.shape[1]
    qb = q_ref[0].astype(jnp.bfloat16)
    zeros = jnp.zeros((own_rows - tp, HEAD_DIM), jnp.float32)
    k_all = jnp.concatenate([kbuf[slot], k_ref[0], zeros], axis=0).astype(jnp.bfloat16)
    v_all = jnp.concatenate([vbuf[slot], v_ref[0], zeros], axis=0).astype(jnp.bfloat16)
    n_cols = n_sel + own_rows
    s = _dot_nt(qb, k_all) * scale

    row = lax.broadcasted_iota(jnp.int32, (tp, n_cols), 0)
    col = lax.broadcasted_iota(jnp.int32, (tp, n_cols), 1)
    seg = col // MOBA_BLOCK
    key_pos = past + (col - n_sel)
    for t in range(tn):
        for r in range(n_top):
            blk = top_ref[((b * heads + h) * tn + t) * n_top + r]
            key_pos = jnp.where(seg == t * n_top + r,
                                blk * MOBA_BLOCK + col % MOBA_BLOCK, key_pos)
    q_pos = past + row
    own = col >= n_sel
    valid = (own & ((col - n_sel) <= row)) | (jnp.logical_not(own) & (seg // n_top == row))
    bias = _t5_bias(jnp.maximum(q_pos - key_pos, 0), bias_ref, h)
    s = jnp.where(valid, s + bias, NEG_INF)
    m = jnp.max(s, axis=1, keepdims=True)
    p = jnp.exp(s - m)
    l = jnp.sum(p, axis=1, keepdims=True)
    o = jnp.dot(p.astype(jnp.bfloat16), v_all, preferred_element_type=jnp.float32)
    o_ref[0] = (o / l).astype(o_ref.dtype)


def _moba_decode(q, k, v, cache_k, cache_v, page_table, top_idx, bias_t, *, layer, heads):
    db, tp, width = q.shape
    tn, n_top = top_idx.shape[-2:]
    past = page_table.shape[1] * PAGE_SIZE
    n_sel = tn * n_top * MOBA_BLOCK
    spec = pl.BlockSpec((1, tp, HEAD_DIM), lambda s, *_: (s // heads, 0, s % heads))
    any_spec = pl.BlockSpec(memory_space=pl.ANY)
    vmem = 2 * 2 * n_sel * HEAD_DIM * 4 + 6 * (n_sel + LANES) * HEAD_DIM * 4
    return pl.pallas_call(
        functools.partial(_moba_decode_body, layer=layer, heads=heads, tn=tn, n_top=n_top,
                          past=past),
        out_shape=jax.ShapeDtypeStruct((db, tp, width), jnp.bfloat16),
        grid_spec=pltpu.PrefetchScalarGridSpec(
            num_scalar_prefetch=3,
            grid=(db * heads,),
            in_specs=[spec, spec, spec, any_spec, any_spec],
            out_specs=spec,
            scratch_shapes=[pltpu.VMEM((2, n_sel, HEAD_DIM), jnp.float32),
                            pltpu.VMEM((2, n_sel, HEAD_DIM), jnp.float32),
                            pltpu.SemaphoreType.DMA((2, 2))],
        ),
        compiler_params=_compiler_params(("arbitrary",), vmem + 8 * 1024 * 1024),
        name="moba_decode",
    )(top_idx.reshape(-1), page_table, bias_t, q, k, v, cache_k, cache_v)


def _hgrn2_gates(qb, fb, layer, lbs):
    q = _silu(qb)
    e = jnp.exp(-jnp.abs(fb))
    t = 1.0 / (1.0 + e)
    et = e * t
    pos = fb >= 0.0
    sig_neg = jnp.where(pos, et, t)
    if layer == 0:
        logf = jnp.minimum(fb, 0.0) - jnp.log(1.0 + e)
        k = sig_neg
    else:
        w = jnp.exp(lbs - jnp.max(lbs, axis=0, keepdims=True))
        p = w / jnp.sum(w, axis=0, keepdims=True)
        lb = jnp.sum(p[:layer + 1], axis=0, keepdims=True) - p[0:1]
        logf = jnp.log(lb + (1.0 - lb) * jnp.where(pos, t, et))
        k = (1.0 - lb) * sig_neg
    return q, logf, k


def _gla_level_ids(c, group):
    t = lax.broadcasted_iota(jnp.int32, (c, c), 0)
    s = lax.broadcasted_iota(jnp.int32, (c, c), 1)
    ids = jnp.full((c, c), -1, jnp.int32)
    size, j = 2 * group, 0
    while size <= c:
        half = size // 2
        hit = (t // size == s // size) & (t % size >= half) & (s % size < half)
        ids = jnp.where(hit, j, ids)
        size, j = 2 * size, j + 1
    return ids


def _gla_chunk(q, k, v, g, st, *, group, tril=None, level_ids=None):
    c, kd = q.shape
    g = g * LOG2_E
    if tril is None:
        row = lax.broadcasted_iota(jnp.int32, q.shape, 0)
        b = g
        sh = 1
        while sh < c:
            b = b + jnp.where(row >= sh, pltpu.roll(b, sh, 0), 0.0)
            sh *= 2
    else:
        hi = g.astype(jnp.bfloat16)
        r1 = g - hi.astype(jnp.float32)
        mid = r1.astype(jnp.bfloat16)
        lo = (r1 - mid.astype(jnp.float32)).astype(jnp.bfloat16)
        b = (jnp.dot(tril, hi, preferred_element_type=jnp.float32)
             + jnp.dot(tril, mid, preferred_element_type=jnp.float32)
             + jnp.dot(tril, lo, preferred_element_type=jnp.float32))

    shape3 = (c // SUBLANES, SUBLANES, kd)
    q3, k3, v3, b3 = (a.reshape(shape3) for a in (q, k, v, b))
    sub = lax.broadcasted_iota(jnp.int32, (1, SUBLANES, kd), 1) % group
    o3 = jnp.sum(q3 * k3, axis=2, keepdims=True) * v3
    for d in range(1, group):
        decay = jnp.exp2(jnp.where(sub >= d, b3 - pltpu.roll(b3, d, 1), NEG_INF))
        w = jnp.sum(q3 * decay * pltpu.roll(k3, d, 1), axis=2, keepdims=True)
        o3 = o3 + w * pltpu.roll(v3, d, 1)
    o = o3.reshape(c, v.shape[1])

    if c > group:
        row = lax.broadcasted_iota(jnp.int32, q.shape, 0)
        scores = jnp.zeros((c, c), jnp.float32)
        size, j = 2 * group, 0
        while size <= c:
            half = size // 2
            bs = b.reshape(c // size, size, kd)
            ref = jnp.broadcast_to(bs[:, half - 1:half, :], bs.shape).reshape(b.shape)
            upper = (row % size) >= half
            diff = b - ref
            x = (jnp.where(upper, q, k) * jnp.exp2(jnp.where(upper, diff, -diff))
                 ).astype(jnp.bfloat16)
            scores = jnp.where(level_ids == j, _dot_nt(x, x), scores)
            size, j = 2 * size, j + 1
        o = o + jnp.dot(scores.astype(jnp.bfloat16), v.astype(jnp.bfloat16),
                        preferred_element_type=jnp.float32)

    b_last = b[c - 1:c, :]
    o = o + _dot_nt((q * jnp.exp2(b)).astype(jnp.bfloat16), st.astype(jnp.bfloat16))
    kw = (k * jnp.exp2(b_last - b)).astype(jnp.bfloat16)
    st_new = st * jnp.exp2(b_last) + lax.dot_general(
        v.astype(jnp.bfloat16), kw, (((0,), (0,)), ((), ())), preferred_element_type=jnp.float32)
    return o, st_new


def _silu(x):
    return x / (1.0 + jnp.exp(-x))


def _hgrn2_out(o, gate, gain):
    ms = jnp.mean(o * o, axis=-1, keepdims=True)
    return o * lax.rsqrt(ms + EPS) * gain * _silu(gate)


def _hgrn2_prompt_body(qb_ref, fb_ref, ib_ref, gb_ref, lb_ref, gain_ref, *refs, layer, n_chunks):
    n_cast = (len(refs) - 3) // 2
    o_ref, s_ref = refs[n_cast:n_cast + 2]
    st_ref = refs[-1]
    _emit_casts(refs[:n_cast], refs[n_cast + 2:-1])
    st_ref[...] = jnp.zeros_like(st_ref)
    lbs = lb_ref[...]
    gain = gain_ref[layer:layer + 1, :]
    tril = (lax.broadcasted_iota(jnp.int32, (GLA_CHUNK, GLA_CHUNK), 0)
            >= lax.broadcasted_iota(jnp.int32, (GLA_CHUNK, GLA_CHUNK), 1)).astype(jnp.bfloat16)
    level_ids = _gla_level_ids(GLA_CHUNK, GLA_GROUP)

    def chunk(i, carry):
        rows = pl.ds(pl.multiple_of(i * GLA_CHUNK, GLA_CHUNK), GLA_CHUNK)
        q, logf, k = _hgrn2_gates(qb_ref[rows, :], fb_ref[rows, :], layer, lbs)
        o, st = _gla_chunk(q, k, ib_ref[rows, :], logf, st_ref[...], group=GLA_GROUP, tril=tril,
                           level_ids=level_ids)
        st_ref[...] = st
        o_ref[rows, :] = _hgrn2_out(o, gb_ref[rows, :], gain).astype(o_ref.dtype)
        return carry

    lax.fori_loop(0, n_chunks, chunk, 0, unroll=4 if n_chunks % 4 == 0 else 1)
    s_ref[0, 0] = st_ref[...].T


def _hgrn2_prompt(proj, lower_logits, gain, *, layer, batch, seq, heads, casts=()):
    m = proj.shape[0]
    assert seq % GLA_CHUNK == 0 and m == batch * seq
    depth = lower_logits.shape[0]

    def col(j):
        return pl.BlockSpec((seq, HG_K), lambda b, h: (b, j * heads + h))

    par = pl.BlockSpec((depth, HG_K), lambda b, h: (0, h))
    gspec = pl.BlockSpec((depth, HG_V), lambda b, h: (0, 0))
    c_in, c_out, c_shapes, c_args, c_vmem = _cast_plan(
        casts, batch * heads, lambda b, h: b * heads + h)
    vmem = (2 * (4 * seq * HG_K * 4 + seq * HG_V * 2) + 16 * GLA_CHUNK * GLA_CHUNK * 4
            + c_vmem)
    return pl.pallas_call(
        functools.partial(_hgrn2_prompt_body, layer=layer, n_chunks=seq // GLA_CHUNK),
        out_shape=[jax.ShapeDtypeStruct((m, heads * HG_V), jnp.bfloat16),
                   jax.ShapeDtypeStruct((batch, heads, HG_K, HG_V), jnp.float32)] + c_shapes,
        grid=(batch, heads),
        in_specs=[col(0), col(1), col(2), col(3), par, gspec] + c_in,
        out_specs=[pl.BlockSpec((seq, HG_V), lambda b, h: (b, h)),
                   pl.BlockSpec((1, 1, HG_K, HG_V), lambda b, h: (b, h, 0, 0))] + c_out,
        scratch_shapes=[pltpu.VMEM((HG_V, HG_K), jnp.float32)],
        compiler_params=_compiler_params(("arbitrary", "arbitrary"), vmem + 8 * 1024 * 1024),
        name="hgrn2_prompt",
    )(proj, proj, proj, proj, lower_logits, gain, *c_args)


def _hgrn2_decode_body(qb_ref, fb_ref, ib_ref, gb_ref, s0_ref, lb_ref, gain_ref, o_ref, s_ref,
                       *, layer, tn):
    keep = lax.broadcasted_iota(jnp.int32, (SUBLANES, HG_K), 0) < tn
    q, logf, k = _hgrn2_gates(qb_ref[0], fb_ref[0], layer, lb_ref[...])
    logf = jnp.where(keep, logf, 0.0)
    k = jnp.where(keep, k, 0.0)
    o, st = _gla_chunk(q, k, ib_ref[0], logf, s0_ref[0, 0].T, group=SUBLANES)
    s_ref[0, 0] = st.T
    o_ref[0] = _hgrn2_out(o, gb_ref[0], gain_ref[layer:layer + 1, :]).astype(o_ref.dtype)


def _hgrn2_decode(proj, s0, lower_logits, gain, *, layer, heads, tn):
    db, tp, _ = proj.shape
    assert tn <= tp == SUBLANES
    depth = lower_logits.shape[0]

    def col(j):
        return pl.BlockSpec((1, tp, HG_K), lambda b, h: (b, 0, j * heads + h))

    sspec = pl.BlockSpec((1, 1, HG_K, HG_V), lambda b, h: (b, h, 0, 0))
    return pl.pallas_call(
        functools.partial(_hgrn2_decode_body, layer=layer, tn=tn),
        out_shape=(jax.ShapeDtypeStruct((db, tp, heads * HG_V), jnp.bfloat16),
                   jax.ShapeDtypeStruct(s0.shape, jnp.float32)),
        grid=(db, heads),
        in_specs=[col(0), col(1), col(2), col(3), sspec,
                  pl.BlockSpec((depth, HG_K), lambda b, h: (0, h)),
                  pl.BlockSpec((depth, HG_V), lambda b, h: (0, 0))],
        out_specs=(pl.BlockSpec((1, tp, HG_V), lambda b, h: (b, 0, h)), sspec),
        compiler_params=_compiler_params(("parallel", "parallel"), 16 * 1024 * 1024),
        name="hgrn2_decode",
    )(proj, proj, proj, proj, s0, lower_logits, gain)


def _trunk_layer(x, layer, wb, norms, moba, hgrn2, kv_stack, *, a_width, b_width):
    g_mix, g_mlp = norms
    d = x[0].shape[1]
    f32, bf16 = jnp.float32, jnp.bfloat16
    h = [_rmsnorm(xi, g_mix[layer], bf16) for xi in x]
    proj = functools.partial(_matmul, h[0], h[1], wb["in", layer])
    qa = proj(col0=0, ncols=a_width, out_dtype=f32, name="proj_q")
    ka = proj(col0=a_width, ncols=a_width, out_dtype=f32, stack=(kv_stack[0], layer),
              name="proj_k")
    va = proj(col0=2 * a_width, ncols=a_width, out_dtype=f32, stack=(kv_stack[1], layer),
              name="proj_v")
    hg = proj(col0=3 * a_width, ncols=4 * b_width, out_dtype=f32, name="proj_hgrn")
    gates = proj(col0=3 * a_width + 4 * b_width, ncols=2 * d, out_dtype=bf16, act=_act_sigmoid,
                 name="proj_gates")
    o_a = [f(q, k, v) for f, q, k, v in zip(moba, qa, ka, va)]
    o_b, s_new = zip(*[f(g) for f, g in zip(hgrn2, hg)])
    merged = _merge_branches(o_a, o_b, wb["ba", layer], wb["bb", layer], gates)
    x = _matmul(*merged, wb["out", layer], out_dtype=f32, residual=x, name="out_proj")
    h2 = [_rmsnorm(xi, g_mlp[layer], bf16) for xi in x]
    u = _matmul(*h2, wb["up", layer], out_dtype=bf16, act=_act_relu2, name="ffn_up")
    x = _matmul(*u, wb["down", layer], out_dtype=f32, residual=x, name="ffn_down")
    return x, ka, va, s_new


def kernel(x_prompt, x_sample, cache_k, cache_v, state_hgrn, page_table, w_in, w_branch_a,
           w_branch_b, w_out, w_up, w_down, norm_mix, norm_mlp, hg_norm, hg_lower_bounds,
           rel_bias, final_norm):
    batch, seq, d = x_prompt.shape
    db, tn, _ = x_sample.shape
    depth = w_in.shape[0]
    a_width = w_branch_a.shape[1]
    b_width = w_branch_b.shape[1]
    a_heads = a_width // HEAD_DIM
    b_heads = b_width // HG_V
    n_pages = page_table.shape[1]
    n_full = (n_pages * PAGE_SIZE) // MOBA_BLOCK
    assert (n_pages * PAGE_SIZE) % MOBA_BLOCK == 0
    n_top = min(MOBA_TOPK, n_full)
    assert n_top > 0 and tn <= SUBLANES

    bf16 = jnp.bfloat16
    bias_t = rel_bias.astype(jnp.float32).T
    tiles = _bias_tiles(bias_t)
    kmeans = _page_block_means(cache_k, page_table)

    x = (x_prompt.reshape(batch * seq, d), x_sample.reshape(db * tn, d))
    w_f32 = {"in": w_in, "ba": w_branch_a, "bb": w_branch_b, "out": w_out, "up": w_up,
             "down": w_down}
    wb = {("in", 0): w_in[0].astype(bf16)}
    outs = {name: [] for name in ("sp", "ks", "vs", "ss")}
    kp = jnp.zeros((depth, batch * seq, a_width), jnp.float32)
    vp = jnp.zeros((depth, batch * seq, a_width), jnp.float32)
    for l in range(depth):
        nxt = [l + 1] if l + 1 < depth else []

        def with_casts(fn, n_out, keys):
            def run(*args):
                res = fn(*args, casts=[(w_f32[name], lyr) for name, lyr in keys])
                wb.update(zip(keys, res[n_out:]))
                return res[0] if n_out == 1 else tuple(res[:n_out])
            return run

        moba_p = with_casts(
            functools.partial(_moba_prompt, bias_t=bias_t, tiles=tiles, layer=l, batch=batch,
                              seq=seq),
            1, [(name, l) for name in ("up", "out", "ba", "bb")])
        hgrn_p = with_casts(
            functools.partial(_hgrn2_prompt, lower_logits=hg_lower_bounds, gain=hg_norm, layer=l,
                              batch=batch, seq=seq, heads=b_heads),
            2, [("down", l)] + [("in", j) for j in nxt])

        def pad_tokens(a):
            return jnp.pad(a.reshape(db, tn, -1), ((0, 0), (0, SUBLANES - tn), (0, 0)))

        def moba_s(q, k, v, l=l):
            q3, k3, v3 = pad_tokens(q), pad_tokens(k), pad_tokens(v)
            top = _decode_topk(q3, kmeans, layer=l, n_top=n_top)[:, :, :tn, :n_top]
            o = _moba_decode(q3, k3, v3, cache_k, cache_v, page_table, top, bias_t, layer=l,
                             heads=a_heads)
            return o[:, :tn].reshape(db * tn, a_width)

        def hgrn_s(hg, l=l):
            o, s = _hgrn2_decode(pad_tokens(hg), state_hgrn[l], hg_lower_bounds, hg_norm,
                                 layer=l, heads=b_heads, tn=tn)
            return o[:, :tn].reshape(db * tn, b_width), s

        x, (kp, ks), (vp, vs), (sp, ss) = _trunk_layer(
            x, l, wb, (norm_mix, norm_mlp), (moba_p, moba_s), (hgrn_p, hgrn_s), (kp, vp),
            a_width=a_width, b_width=b_width)
        outs["sp"].append(sp)
        outs["ks"].append(ks.reshape(db, tn, a_heads, HEAD_DIM))
        outs["vs"].append(vs.reshape(db, tn, a_heads, HEAD_DIM))
        outs["ss"].append(ss)

    y_prompt = _rmsnorm(x[0], final_norm, jnp.float32).reshape(batch, seq, d)
    y_sample = _rmsnorm(x[1], final_norm, jnp.float32).reshape(db, tn, d)
    kv_shape = (depth, batch, seq, a_heads, HEAD_DIM)
    return (y_prompt, y_sample, kp.reshape(kv_shape), vp.reshape(kv_shape),
            jnp.stack(outs["sp"]), jnp.stack(outs["ks"]), jnp.stack(outs["vs"]),
            jnp.stack(outs["ss"]))
```
